```python
import math
import jax, jax.numpy as jnp
from jax import lax
import numpy as np

D_MODEL = 2048
BATCH = 1
SEQ = 16384
DEPTH = 2
DEC_BATCH = 2
DEC_SEQ = 8192
PAST_LEN = 128

N_EVEN = (DEPTH + 1) // 2
N_ODD = DEPTH // 2
CONV_CH = D_MODEL // 2
CONV_WIDTH = 31
DA_HEAD_DIM = 128
DA_HEADS = D_MODEL // (4 * DA_HEAD_DIM)
DA_QK = DA_HEADS * 2 * DA_HEAD_DIM
DA_V = DA_HEADS * 2 * DA_HEAD_DIM
EVEN_IN = 2 * CONV_CH + 2 * DA_QK + DA_V
ML_HEADS = 4
ML_DV = D_MODEL // ML_HEADS
ML_DK = ML_DV // 2
ML_GATES = 4 * ML_HEADS
ODD_IN = 2 * ML_HEADS * ML_DK + 2 * ML_HEADS * ML_DV + ML_GATES
FFN_HIDDEN = ((8 * D_MODEL // 3 + 255) // 256) * 256
CHUNK = 128
QBLK = 128
ROPE_THETA = 10000.0
EPS = 1e-6

kernel_name = "hybrid_conv_diffattn_bimlstm_encoder"


def rms_norm(x, g):
    xf = x.astype(jnp.float32)
    y = xf * lax.rsqrt(jnp.mean(xf * xf, axis=-1, keepdims=True) + EPS)
    return (y * g.astype(jnp.float32)).astype(x.dtype)


def apply_rotary(x):
    s, d = x.shape[1], x.shape[-1]
    inv = 1.0 / (ROPE_THETA ** (jnp.arange(0, d, 2, dtype=jnp.float32) / d))
    ang = jnp.arange(s, dtype=jnp.float32)[:, None] * inv[None, :]
    cos = jnp.cos(ang)[None, :, None, None, :]
    sin = jnp.sin(ang)[None, :, None, None, :]
    xf = x.astype(jnp.float32)
    x1, x2 = xf[..., : d // 2], xf[..., d // 2:]
    return jnp.concatenate([x1 * cos - x2 * sin, x2 * cos + x1 * sin], axis=-1).astype(x.dtype)


def conformer_conv(a, g, conv_w, conv_b, conv_norm):
    u = a * jax.nn.sigmoid(g)
    w = conv_w.astype(u.dtype)[:, None, :]
    pad = CONV_WIDTH // 2
    y = lax.conv_general_dilated(u, w, window_strides=(1,), padding=[(pad, pad)],
                                 dimension_numbers=("NWC", "WIO", "NWC"),
                                 feature_group_count=CONV_CH)
    y = y + conv_b.astype(y.dtype)
    return jax.nn.silu(rms_norm(y, conv_norm))


def diff_attention(q, k, v, q_norm, k_norm, lam_params, subln, lambda_init):
    b, s, h, _, d = q.shape
    q = apply_rotary(rms_norm(q, q_norm))
    k = apply_rotary(rms_norm(k, k_norm))
    lp = lam_params.astype(jnp.float32)
    lam = jnp.exp(jnp.sum(lp[0] * lp[1])) - jnp.exp(jnp.sum(lp[2] * lp[3])) + lambda_init
    scale = 1.0 / math.sqrt(d)
    nb = s // QBLK
    qb = q.reshape(b, nb, QBLK, h, 2, d).transpose(1, 0, 2, 3, 4, 5)

    def block(qi):
        sc = jnp.einsum("bqhcd,bkhcd->bhcqk", qi, k,
                        preferred_element_type=jnp.float32) * scale
        p = jax.nn.softmax(sc, axis=-1)
        a = p[:, :, 0] - lam * p[:, :, 1]
        return jnp.einsum("bhqk,bkhe->bqhe", a.astype(v.dtype), v)

    o = lax.map(block, qb)
    o = o.transpose(1, 0, 2, 3, 4).reshape(b, s, h, 2 * d)
    o = rms_norm(o, subln) * (1.0 - lambda_init)
    return o.reshape(b, s, h * 2 * d).astype(v.dtype)


def even_mixer(xn, w_in, conv_w, conv_b, conv_norm, q_norm, k_norm, lam_params, subln, w_out, lambda_init):
    b, s, _ = xn.shape
    hcat = xn @ w_in.astype(xn.dtype)
    c0 = CONV_CH
    ca, cg, q, k, v = jnp.split(hcat, [c0, 2 * c0, 2 * c0 + DA_QK, 2 * c0 + 2 * DA_QK], axis=-1)
    y_conv = conformer_conv(ca, cg, conv_w, conv_b, conv_norm)
    q = q.reshape(b, s, DA_HEADS, 2, DA_HEAD_DIM)
    k = k.reshape(b, s, DA_HEADS, 2, DA_HEAD_DIM)
    v = v.reshape(b, s, DA_HEADS, 2 * DA_HEAD_DIM)
    y_att = diff_attention(q, k, v, q_norm, k_norm, lam_params, subln, lambda_init)
    return jnp.concatenate([y_conv, y_att], axis=-1) @ w_out.astype(xn.dtype)


def mlstm_chunked(q, k, v, ig, lf):
    b, s, h, dk = q.shape
    dv = v.shape[-1]
    nc = s // CHUNK
    c4 = lambda t: t.reshape(b, nc, CHUNK, h, t.shape[-1]).transpose(1, 0, 3, 2, 4)
    c3 = lambda t: t.reshape(b, nc, CHUNK, h).transpose(1, 0, 3, 2)
    mask = jnp.tril(jnp.ones((CHUNK, CHUNK), dtype=bool))

    def step(carry, inp):
        cmat, nvec, m = carry
        qc, kc, vc, ic, fc = inp
        bcum = jnp.cumsum(fc, axis=-1)
        inter = bcum + m[..., None]
        dlog = bcum[..., :, None] - bcum[..., None, :] + ic[..., None, :]
        dlog = jnp.where(mask, dlog, -jnp.inf)
        mt = jnp.maximum(inter, jnp.max(dlog, axis=-1))
        sc = jnp.einsum("bhtd,bhsd->bhts", qc, kc) * jnp.exp(dlog - mt[..., None])
        w_int = jnp.exp(inter - mt)
        num = jnp.einsum("bhts,bhsv->bhtv", sc, vc) + w_int[..., None] * jnp.einsum("bhtd,bhdv->bhtv", qc, cmat)
        den = jnp.sum(sc, axis=-1) + w_int * jnp.einsum("bhtd,bhd->bht", qc, nvec)
        hout = num / jnp.maximum(jnp.abs(den), jnp.exp(-mt))[..., None]
        m_new = mt[..., -1]
        w_s = jnp.exp(bcum[..., -1:] - bcum + ic - m_new[..., None])
        decay = jnp.exp(bcum[..., -1] + m - m_new)
        kw = kc * w_s[..., None]
        c_new = decay[..., None, None] * cmat + jnp.einsum("bhsd,bhsv->bhdv", kw, vc)
        n_new = decay[..., None] * nvec + jnp.sum(kw, axis=2)
        return (c_new, n_new, m_new), hout

    init = (jnp.zeros((b, h, dk, dv), jnp.float32), jnp.zeros((b, h, dk), jnp.float32),
            jnp.zeros((b, h), jnp.float32))
    _, hs = lax.scan(step, init, (c4(q), c4(k), c4(v), c3(ig), c3(lf)))
    return hs.transpose(1, 0, 3, 2, 4).reshape(b, s, h, dv)


def odd_mixer(xn, w_in, gate_b, out_norm, w_out):
    b, s, _ = xn.shape
    hcat = xn @ w_in.astype(xn.dtype)
    nq = ML_HEADS * ML_DK
    nv = ML_HEADS * ML_DV
    q, k, v, o, g = jnp.split(hcat, [nq, 2 * nq, 2 * nq + nv, 2 * nq + 2 * nv], axis=-1)
    f32 = jnp.float32
    q = q.reshape(b, s, ML_HEADS, ML_DK).astype(f32)
    k = k.reshape(b, s, ML_HEADS, ML_DK).astype(f32) * (1.0 / math.sqrt(ML_DK))
    v = v.reshape(b, s, ML_HEADS, ML_DV).astype(f32)
    g = g.astype(f32).reshape(b, s, 4, ML_HEADS) + gate_b.astype(f32)[None, None]
    i_f, lf_f = g[:, :, 0], jax.nn.log_sigmoid(g[:, :, 1])
    i_b, lf_b = g[:, :, 2], jax.nn.log_sigmoid(g[:, :, 3])
    h_fwd = mlstm_chunked(q, k, v, i_f, lf_f)
    flip = lambda t: jnp.flip(t, axis=1)
    h_bwd = flip(mlstm_chunked(flip(q), flip(k), flip(v), flip(i_b), flip(lf_b)))
    hsum = rms_norm(h_fwd + h_bwd, out_norm).reshape(b, s, nv).astype(xn.dtype)
    return (hsum * jax.nn.sigmoid(o)) @ w_out.astype(xn.dtype)


def swiglu(xn, wg, wu, wd):
    dt = xn.dtype
    return (jax.nn.silu(xn @ wg.astype(dt)) * (xn @ wu.astype(dt))) @ wd.astype(dt)


def trunk(x, norm_mix, norm_ffn, ev_w_in, ev_conv_w, ev_conv_b, ev_conv_norm, ev_q_norm, ev_k_norm,
          ev_lambda, ev_subln, ev_w_out, od_w_in, od_gate_b, od_out_norm, od_w_out,
          ffn_w_gate, ffn_w_up, ffn_w_down):
    for i in range(DEPTH):
        xn = rms_norm(x, norm_mix[i])
        if i % 2 == 0:
            j = i // 2
            lambda_init = 0.8 - 0.6 * math.exp(-0.3 * i)
            y = even_mixer(xn, ev_w_in[j], ev_conv_w[j], ev_conv_b[j], ev_conv_norm[j], ev_q_norm[j],
                           ev_k_norm[j], ev_lambda[j], ev_subln[j], ev_w_out[j], lambda_init)
        else:
            j = i // 2
            y = odd_mixer(xn, od_w_in[j], od_gate_b[j], od_out_norm[j], od_w_out[j])
        x = x + y.astype(x.dtype)
        x = x + swiglu(rms_norm(x, norm_ffn[i]), ffn_w_gate[i], ffn_w_up[i], ffn_w_down[i]).astype(x.dtype)
    return x


def setup_inputs(seed: int = 0) -> dict:
    key = jax.random.key(seed)
    ks = jax.random.split(key, 24)
    f32 = jnp.float32
    nrm = lambda k, shape, scale: jax.random.normal(k, shape, f32) * scale
    out_scale = D_MODEL ** -0.5 / math.sqrt(2.0)
    f_bias = jnp.linspace(3.0, 6.0, ML_HEADS, dtype=f32)
    gate_b = jnp.stack([jnp.zeros((ML_HEADS,), f32), f_bias, jnp.zeros((ML_HEADS,), f32), f_bias])
    od_gate_b = gate_b[None] + nrm(ks[13], (N_ODD, 4, ML_HEADS), 0.1)
    return {
        "x_prompt": nrm(ks[0], (BATCH, SEQ, D_MODEL), 1.0),
        "x_sample": nrm(ks[1], (DEC_BATCH, DEC_SEQ, D_MODEL), 1.0),
        "norm_mix": 1.0 + nrm(ks[2], (DEPTH, D_MODEL), 0.02),
        "norm_ffn": 1.0 + nrm(ks[3], (DEPTH, D_MODEL), 0.02),
        "ev_w_in": nrm(ks[4], (N_EVEN, D_MODEL, EVEN_IN), D_MODEL ** -0.5),
        "ev_conv_w": nrm(ks[5], (N_EVEN, CONV_WIDTH, CONV_CH), CONV_WIDTH ** -0.5),
        "ev_conv_b": nrm(ks[6], (N_EVEN, CONV_CH), 0.01),
        "ev_conv_norm": 1.0 + nrm(ks[7], (N_EVEN, CONV_CH), 0.02),
        "ev_q_norm": 1.0 + nrm(ks[8], (N_EVEN, DA_HEAD_DIM), 0.02),
        "ev_k_norm": 1.0 + nrm(ks[9], (N_EVEN, DA_HEAD_DIM), 0.02),
        "ev_lambda": nrm(ks[10], (N_EVEN, 4, DA_HEAD_DIM), 0.1),
        "ev_subln": 1.0 + nrm(ks[11], (N_EVEN, 2 * DA_HEAD_DIM), 0.02),
        "ev_w_out": nrm(ks[12], (N_EVEN, D_MODEL, D_MODEL), out_scale),
        "od_w_in": nrm(ks[14], (N_ODD, D_MODEL, ODD_IN), D_MODEL ** -0.5),
        "od_gate_b": od_gate_b,
        "od_out_norm": 1.0 + nrm(ks[15], (N_ODD, ML_HEADS, ML_DV), 0.02),
        "od_w_out": nrm(ks[16], (N_ODD, D_MODEL, D_MODEL), out_scale),
        "ffn_w_gate": nrm(ks[17], (DEPTH, D_MODEL, FFN_HIDDEN), D_MODEL ** -0.5),
        "ffn_w_up": nrm(ks[18], (DEPTH, D_MODEL, FFN_HIDDEN), D_MODEL ** -0.5),
        "ffn_w_down": nrm(ks[19], (DEPTH, FFN_HIDDEN, D_MODEL), FFN_HIDDEN ** -0.5 / math.sqrt(2.0)),
    }


def reference(x_prompt, x_sample, norm_mix, norm_ffn, ev_w_in, ev_conv_w, ev_conv_b, ev_conv_norm,
              ev_q_norm, ev_k_norm, ev_lambda, ev_subln, ev_w_out, od_w_in, od_gate_b, od_out_norm,
              od_w_out, ffn_w_gate, ffn_w_up, ffn_w_down):
    y_prompt = trunk(x_prompt, norm_mix, norm_ffn, ev_w_in, ev_conv_w, ev_conv_b, ev_conv_norm, ev_q_norm,
                     ev_k_norm, ev_lambda, ev_subln, ev_w_out, od_w_in, od_gate_b, od_out_norm, od_w_out,
                     ffn_w_gate, ffn_w_up, ffn_w_down)
    y_sample = trunk(x_sample, norm_mix, norm_ffn, ev_w_in, ev_conv_w, ev_conv_b, ev_conv_norm, ev_q_norm,
                     ev_k_norm, ev_lambda, ev_subln, ev_w_out, od_w_in, od_gate_b, od_out_norm, od_w_out,
                     ffn_w_gate, ffn_w_up, ffn_w_down)
    return (y_prompt, y_sample)
```

```python
import functools
import math

import jax
import jax.numpy as jnp
from jax import lax
from jax.experimental import pallas as pl
from jax.experimental.pallas import tpu as pltpu

F32 = jnp.float32
BF16 = jnp.bfloat16

D_MODEL = 2048
CONV_CH = D_MODEL // 2
CONV_WIDTH = 31
CONV_PAD = CONV_WIDTH // 2
CONV_HALO = 16
DA_HEAD_DIM = 128
DA_HEADS = D_MODEL // (4 * DA_HEAD_DIM)
DA_QK = DA_HEADS * 2 * DA_HEAD_DIM
ML_HEADS = 4
ML_DV = D_MODEL // ML_HEADS
ML_DK = ML_DV // 2
ML_GATES = 4 * ML_HEADS
ML_MAIN = 2 * ML_HEADS * ML_DK + 2 * ML_HEADS * ML_DV
FFN_HIDDEN = ((8 * D_MODEL // 3 + 255) // 256) * 256
CHUNK = 128
ROPE_THETA = 10000.0
EPS = 1e-6
LANES = 128
VMEM_LIMIT = 56 * 1024 * 1024


def _params(semantics):
    return pltpu.CompilerParams(dimension_semantics=semantics, vmem_limit_bytes=VMEM_LIMIT)


def _rms(x, g):
    ms = jnp.mean(x * x, axis=-1, keepdims=True)
    return x * lax.rsqrt(ms + EPS) * g


def _norm_matmul_kernel(x_ref, g_ref, w_ref, o_ref, xn_ref):
    @pl.when(pl.program_id(1) == 0)
    def _():
        xn_ref[...] = _rms(x_ref[...], g_ref[...]).astype(BF16)

    o_ref[...] = jnp.dot(xn_ref[...], w_ref[...], preferred_element_type=F32).astype(o_ref.dtype)


def norm_matmul(x, g, w, out_dtype, tm=512, tn=1024):
    m, d = x.shape
    n = w.shape[1]
    tn = min(tn, n)
    return pl.pallas_call(
        _norm_matmul_kernel,
        grid=(m // tm, n // tn),
        in_specs=[pl.BlockSpec((tm, d), lambda i, j: (i, 0)),
                  pl.BlockSpec((1, d), lambda i, j: (0, 0)),
                  pl.BlockSpec((d, tn), lambda i, j: (0, j))],
        out_specs=pl.BlockSpec((tm, tn), lambda i, j: (i, j)),
        out_shape=jax.ShapeDtypeStruct((m, n), out_dtype),
        scratch_shapes=[pltpu.VMEM((tm, d), BF16)],
        compiler_params=_params(("parallel", "arbitrary")),
        name="norm_matmul",
    )(x, g, w)


def _conv_kernel(ac_ref, ap_ref, an_ref, gc_ref, gp_ref, gn_ref, w_ref, b_ref, nrm_ref, o_ref, u_ref,
                 *, ts, tr):
    i = pl.program_id(1)
    n = pl.num_programs(1)

    def glu(a_ref, g_ref):
        return a_ref[0].astype(F32) * jax.nn.sigmoid(g_ref[0].astype(F32))

    u_ref[0:CONV_HALO, :] = jnp.where(i > 0, glu(ap_ref, gp_ref), 0.0)
    u_ref[CONV_HALO:CONV_HALO + ts, :] = glu(ac_ref, gc_ref)
    u_ref[CONV_HALO + ts:, :] = jnp.where(i < n - 1, glu(an_ref, gn_ref), 0.0)

    bias = b_ref[...]
    nrm = nrm_ref[...]
    base = CONV_HALO - CONV_PAD
    for r in range(ts // tr):
        acc = jnp.zeros((tr, CONV_CH), F32)
        for k in range(CONV_WIDTH):
            lo = base + r * tr + k
            acc = acc + u_ref[lo:lo + tr, :] * w_ref[k:k + 1, :]
        y = _rms(acc + bias, nrm)
        o_ref[0, r * tr:(r + 1) * tr, :] = (y * jax.nn.sigmoid(y)).astype(o_ref.dtype)


def conv_module(hcat, conv_w, conv_b, conv_norm, ts=256, tr=32):
    b, s, _ = hcat.shape
    nh = ts // CONV_HALO
    last = s // CONV_HALO - 1
    cur = lambda col: pl.BlockSpec((1, ts, CONV_CH), lambda bi, i: (bi, i, col))
    prev = lambda col: pl.BlockSpec((1, CONV_HALO, CONV_CH),
                                    lambda bi, i: (bi, jnp.maximum(i * nh - 1, 0), col))
    nxt = lambda col: pl.BlockSpec((1, CONV_HALO, CONV_CH),
                                   lambda bi, i: (bi, jnp.minimum((i + 1) * nh, last), col))
    vec = lambda rows: pl.BlockSpec((rows, CONV_CH), lambda bi, i: (0, 0))
    return pl.pallas_call(
        functools.partial(_conv_kernel, ts=ts, tr=tr),
        grid=(b, s // ts),
        in_specs=[cur(0), prev(0), nxt(0), cur(1), prev(1), nxt(1), vec(CONV_WIDTH), vec(1), vec(1)],
        out_specs=pl.BlockSpec((1, ts, CONV_CH), lambda bi, i: (bi, i, 0)),
        out_shape=jax.ShapeDtypeStruct((b, s, CONV_CH), BF16),
        scratch_shapes=[pltpu.VMEM((ts + 2 * CONV_HALO, CONV_CH), F32)],
        compiler_params=_params(("parallel", "arbitrary")),
        name="conv_module",
    )(hcat, hcat, hcat, hcat, hcat, hcat, conv_w, conv_b, conv_norm)


def _qk_prep_kernel(q_ref, k_ref, cos_ref, sin_ref, qn_ref, kn_ref, qo_ref, ko_ref, *, scale):
    cos = cos_ref[...]
    sin = sin_ref[...]

    def prep(x_ref, g, o_ref, mul):
        for j in range(DA_QK // DA_HEAD_DIM):
            sl = slice(j * DA_HEAD_DIM, (j + 1) * DA_HEAD_DIM)
            xn = _rms(x_ref[0, :, sl].astype(F32), g)
            rot = pltpu.roll(xn, DA_HEAD_DIM // 2, axis=1)
            o_ref[0, :, sl] = ((xn * cos + rot * sin) * mul).astype(o_ref.dtype)

    prep(q_ref, qn_ref[...], qo_ref, scale)
    prep(k_ref, kn_ref[...], ko_ref, 1.0)


def qk_prep(hcat, cos, sin, q_norm, k_norm, ts=512):
    b, s, _ = hcat.shape
    col = lambda c: pl.BlockSpec((1, ts, DA_QK), lambda bi, i: (bi, i, c))
    tab = pl.BlockSpec((ts, DA_HEAD_DIM), lambda bi, i: (i, 0))
    vec = pl.BlockSpec((1, DA_HEAD_DIM), lambda bi, i: (0, 0))
    out = pl.BlockSpec((1, ts, DA_QK), lambda bi, i: (bi, i, 0))
    return pl.pallas_call(
        functools.partial(_qk_prep_kernel, scale=1.0 / math.sqrt(DA_HEAD_DIM)),
        grid=(b, s // ts),
        in_specs=[col(2), col(3), tab, tab, vec, vec],
        out_specs=[out, out],
        out_shape=[jax.ShapeDtypeStruct((b, s, DA_QK), BF16)] * 2,
        compiler_params=_params(("parallel", "parallel")),
        name="qk_prep",
    )(hcat, hcat, cos, sin, q_norm, k_norm)


def _attn_kernel(q_ref, k_ref, v_ref, lam_ref, subln_ref, o_ref, m_ref, l_ref, acc_ref, *, lambda_init):
    ki = pl.program_id(3)
    d = DA_HEAD_DIM

    @pl.when(ki == 0)
    def _():
        m_ref[...] = jnp.full_like(m_ref, -jnp.inf)
        l_ref[...] = jnp.zeros_like(l_ref)
        acc_ref[...] = jnp.zeros_like(acc_ref)

    v = v_ref[0]
    for c in range(2):
        q = q_ref[0, :, c * d:(c + 1) * d]
        k = k_ref[0, :, c * d:(c + 1) * d]
        s = lax.dot_general(q, k, (((1,), (1,)), ((), ())), preferred_element_type=F32)
        m_prev = m_ref[c]
        m_new = jnp.maximum(m_prev, jnp.max(s, axis=-1, keepdims=True))
        alpha = jnp.exp(m_prev - m_new)
        p = jnp.exp(s - m_new)
        l_ref[c] = alpha * l_ref[c] + jnp.sum(p, axis=-1, keepdims=True)
        acc_ref[c] = alpha * acc_ref[c] + jnp.dot(p.astype(BF16), v, preferred_element_type=F32)
        m_ref[c] = m_new

    @pl.when(ki == pl.num_programs(3) - 1)
    def _():
        lp = lam_ref[...]
        lam = (jnp.exp(jnp.sum(lp[0:1] * lp[1:2], axis=-1, keepdims=True))
               - jnp.exp(jnp.sum(lp[2:3] * lp[3:4], axis=-1, keepdims=True)) + lambda_init)
        o = acc_ref[0] / l_ref[0] - lam * (acc_ref[1] / l_ref[1])
        o_ref[0] = (_rms(o, subln_ref[...]) * (1.0 - lambda_init)).astype(o_ref.dtype)


def diff_attention(qh, kh, hcat, lam_params, subln, lambda_init, tq=512, tk=1024):
    b, s, _ = qh.shape
    dv = 2 * DA_HEAD_DIM
    v_col0 = (hcat.shape[2] - DA_QK) // dv
    return pl.pallas_call(
        functools.partial(_attn_kernel, lambda_init=lambda_init),
        grid=(b, DA_HEADS, s // tq, s // tk),
        in_specs=[pl.BlockSpec((1, tq, dv), lambda bi, h, qi, ki: (bi, qi, h)),
                  pl.BlockSpec((1, tk, dv), lambda bi, h, qi, ki: (bi, ki, h)),
                  pl.BlockSpec((1, tk, dv), lambda bi, h, qi, ki: (bi, ki, v_col0 + h)),
                  pl.BlockSpec((4, DA_HEAD_DIM), lambda bi, h, qi, ki: (0, 0)),
                  pl.BlockSpec((1, dv), lambda bi, h, qi, ki: (0, 0))],
        out_specs=pl.BlockSpec((1, tq, dv), lambda bi, h, qi, ki: (bi, qi, h)),
        out_shape=jax.ShapeDtypeStruct((b, s, DA_QK), BF16),
        scratch_shapes=[pltpu.VMEM((2, tq, 1), F32), pltpu.VMEM((2, tq, 1), F32),
                        pltpu.VMEM((2, tq, dv), F32)],
        compiler_params=_params(("parallel", "parallel", "parallel", "arbitrary")),
        name="diff_attention",
    )(qh, kh, hcat, lam_params, subln)


def _even_out_kernel(x_ref, a_ref, b_ref, w_ref, o_ref):
    ka = a_ref.shape[1]
    y = jnp.dot(a_ref[...], w_ref[0:ka, :], preferred_element_type=F32)
    y = y + jnp.dot(b_ref[...], w_ref[ka:, :], preferred_element_type=F32)
    o_ref[...] = x_ref[...] + y


def even_out(x, y_conv, y_att, w, tm=512):
    m, d = x.shape
    row = lambda width: pl.BlockSpec((tm, width), lambda i: (i, 0))
    return pl.pallas_call(
        _even_out_kernel,
        grid=(m // tm,),
        in_specs=[row(d), row(y_conv.shape[1]), row(y_att.shape[1]),
                  pl.BlockSpec(w.shape, lambda i: (0, 0))],
        out_specs=row(d),
        out_shape=jax.ShapeDtypeStruct((m, d), F32),
        compiler_params=_params(("parallel",)),
        name="even_out",
    )(x, y_conv, y_att, w)


def _odd_out_kernel(x_ref, hf_ref, hb_ref, og_ref, nrm_ref, w_ref, o_ref, z_ref):
    for h in range(ML_HEADS):
        sl = slice(h * ML_DV, (h + 1) * ML_DV)
        hs = hf_ref[:, sl].astype(F32) + hb_ref[:, sl].astype(F32)
        z = _rms(hs, nrm_ref[:, sl]) * jax.nn.sigmoid(og_ref[:, sl].astype(F32))
        z_ref[:, sl] = z.astype(BF16)
    o_ref[...] = x_ref[...] + jnp.dot(z_ref[...], w_ref[...], preferred_element_type=F32)


def odd_out(x, h_fwd, h_bwd, hcat, out_norm, w, tm=512):
    m, d = x.shape
    row = lambda c: pl.BlockSpec((tm, d), lambda i: (i, c))
    return pl.pallas_call(
        _odd_out_kernel,
        grid=(m // tm,),
        in_specs=[row(0), row(0), row(0), row(2),
                  pl.BlockSpec((1, d), lambda i: (0, 0)),
                  pl.BlockSpec(w.shape, lambda i: (0, 0))],
        out_specs=row(0),
        out_shape=jax.ShapeDtypeStruct((m, d), F32),
        scratch_shapes=[pltpu.VMEM((tm, d), BF16)],
        compiler_params=_params(("parallel",)),
        name="odd_out",
    )(x, h_fwd, h_bwd, hcat, out_norm, w)


def _ffn_kernel(x_ref, g_ref, wg_ref, wu_ref, wd_ref, o_ref, xn_ref):
    j = pl.program_id(1)

    @pl.when(j == 0)
    def _():
        x = x_ref[...]
        xn_ref[...] = _rms(x, g_ref[...]).astype(BF16)
        o_ref[...] = x

    xn = xn_ref[...]
    gate = jnp.dot(xn, wg_ref[...], preferred_element_type=F32)
    up = jnp.dot(xn, wu_ref[...], preferred_element_type=F32)
    hid = (gate * jax.nn.sigmoid(gate) * up).astype(BF16)
    o_ref[...] += jnp.dot(hid, wd_ref[...], preferred_element_type=F32)


def ffn(x, g, wg, wu, wd, tm=512, tf=512):
    m, d = x.shape
    f = wg.shape[1]
    return pl.pallas_call(
        _ffn_kernel,
        grid=(m // tm, f // tf),
        in_specs=[pl.BlockSpec((tm, d), lambda i, j: (i, 0)),
                  pl.BlockSpec((1, d), lambda i, j: (0, 0)),
                  pl.BlockSpec((d, tf), lambda i, j: (0, j)),
                  pl.BlockSpec((d, tf), lambda i, j: (0, j)),
                  pl.BlockSpec((tf, d), lambda i, j: (j, 0))],
        out_specs=pl.BlockSpec((tm, d), lambda i, j: (i, 0)),
        out_shape=jax.ShapeDtypeStruct((m, d), F32),
        scratch_shapes=[pltpu.VMEM((tm, d), BF16)],
        compiler_params=_params(("parallel", "arbitrary")),
        name="ffn",
    )(x, g, wg, wu, wd)


def _split3(x):
    x1 = x.astype(BF16)
    r = x - x1.astype(F32)
    x2 = r.astype(BF16)
    x3 = (r - x2.astype(F32)).astype(BF16)
    return x1, x2, x3


def _mlstm_kernel(q_ref, k_ref, v_ref, g_ref, gt_ref, bc_ref, br_ref, h_ref, c_ref, n_ref, m_ref,
                  *, reverse):
    L = CHUNK

    @pl.when(pl.program_id(1) == 0)
    def _():
        c_ref[...] = jnp.zeros_like(c_ref)
        n_ref[...] = jnp.zeros_like(n_ref)
        m_ref[...] = jnp.zeros_like(m_ref)

    t_idx = lax.broadcasted_iota(jnp.int32, (L, L), 0)
    s_idx = lax.broadcasted_iota(jnp.int32, (L, L), 1)
    mask = (s_idx >= t_idx) if reverse else (s_idx <= t_idx)
    mask_t = (s_idx <= t_idx) if reverse else (s_idx >= t_idx)
    tri_col = jnp.where(mask, 1.0, 0.0).astype(BF16)
    tri_row = jnp.where(mask_t, 1.0, 0.0).astype(BF16)
    last = 0 if reverse else L - 1

    g_col = g_ref[0][:, 0:ML_GATES] + br_ref[...]
    g_row = gt_ref[0] + bc_ref[...]
    lf_col = jax.nn.log_sigmoid(g_col)
    lf_row = jax.nn.log_sigmoid(g_row)
    bcum_col = sum(jnp.dot(tri_col, p, preferred_element_type=F32) for p in _split3(lf_col))
    bcum_row = sum(jnp.dot(p, tri_row, preferred_element_type=F32) for p in _split3(lf_row))

    it = 2 if reverse else 0
    for h in range(ML_HEADS):
        ci = it * ML_HEADS + h
        cf = (it + 1) * ML_HEADS + h
        i_col = g_col[:, ci:ci + 1]
        i_row = g_row[ci:ci + 1, :]
        b_col = bcum_col[:, cf:cf + 1]
        b_row = bcum_row[cf:cf + 1, :]
        total = b_col[last:last + 1, :]
        m_prev = m_ref[h]

        q = q_ref[0, :, h * ML_DK:(h + 1) * ML_DK]
        k = k_ref[0, :, h * ML_DK:(h + 1) * ML_DK]
        v = v_ref[0, :, h * ML_DV:(h + 1) * ML_DV]

        dlog = jnp.where(mask, b_col - b_row + i_row, -jnp.inf)
        inter = b_col + m_prev
        mt = jnp.maximum(inter, jnp.max(dlog, axis=-1, keepdims=True))
        dmat = jnp.exp(dlog - mt)
        qk = lax.dot_general(q, k, (((1,), (1,)), ((), ())), preferred_element_type=F32)
        sc = qk * dmat * (1.0 / math.sqrt(ML_DK))
        w_int = jnp.exp(inter - mt)
        qc = jnp.dot(q, c_ref[h].astype(BF16), preferred_element_type=F32)
        num = jnp.dot(sc.astype(BF16), v, preferred_element_type=F32) + w_int * qc
        qn = jnp.sum(q.astype(F32) * n_ref[h], axis=-1, keepdims=True)
        den = jnp.sum(sc, axis=-1, keepdims=True) + w_int * qn
        hout = num / jnp.maximum(jnp.abs(den), jnp.exp(-mt))
        h_ref[0, :, h * ML_DV:(h + 1) * ML_DV] = hout.astype(h_ref.dtype)

        m_new = mt[last:last + 1, :]
        w_s = jnp.exp(total - b_col + i_col - m_new) * (1.0 / math.sqrt(ML_DK))
        decay = jnp.exp(total + m_prev - m_new)
        kw = k.astype(F32) * w_s
        c_ref[h] = decay * c_ref[h] + jnp.dot(kw.T.astype(BF16), v, preferred_element_type=F32)
        n_ref[h] = decay * n_ref[h] + jnp.sum(kw, axis=0, keepdims=True)
        m_ref[h] = m_new


def mlstm_scan(hcat, gates, gates_t, bias_col, bias_row, reverse):
    b, s, _ = hcat.shape
    nc = s // CHUNK
    pos = (lambda c: nc - 1 - c) if reverse else (lambda c: c)
    nq = ML_HEADS * ML_DK
    nv = ML_HEADS * ML_DV
    return pl.pallas_call(
        functools.partial(_mlstm_kernel, reverse=reverse),
        grid=(b, nc),
        in_specs=[pl.BlockSpec((1, CHUNK, nq), lambda bi, c: (bi, pos(c), 0)),
                  pl.BlockSpec((1, CHUNK, nq), lambda bi, c: (bi, pos(c), 1)),
                  pl.BlockSpec((1, CHUNK, nv), lambda bi, c: (bi, pos(c), 1)),
                  pl.BlockSpec((1, CHUNK, LANES), lambda bi, c: (bi, pos(c), 0)),
                  pl.BlockSpec((1, ML_GATES, CHUNK), lambda bi, c: (bi, 0, pos(c))),
                  pl.BlockSpec((ML_GATES, 1), lambda bi, c: (0, 0)),
                  pl.BlockSpec((1, ML_GATES), lambda bi, c: (0, 0))],
        out_specs=pl.BlockSpec((1, CHUNK, nv), lambda bi, c: (bi, pos(c), 0)),
        out_shape=jax.ShapeDtypeStruct((b, s, nv), BF16),
        scratch_shapes=[pltpu.VMEM((ML_HEADS, ML_DK, ML_DV), F32),
                        pltpu.VMEM((ML_HEADS, 1, ML_DK), F32),
                        pltpu.VMEM((ML_HEADS, 1, 1), F32)],
        compiler_params=_params(("parallel", "arbitrary")),
        name="mlstm_bwd" if reverse else "mlstm_fwd",
    )(hcat, hcat, hcat, gates, gates_t, bias_col, bias_row)


def _rope_tables(s):
    d = DA_HEAD_DIM
    inv = 1.0 / (ROPE_THETA ** (jnp.arange(0, d, 2, dtype=F32) / d))
    ang = jnp.arange(s, dtype=F32)[:, None] * inv[None, :]
    cos, sin = jnp.cos(ang), jnp.sin(ang)
    return jnp.concatenate([cos, cos], axis=-1), jnp.concatenate([-sin, sin], axis=-1)


def _even_layer(x, b, s, p):
    m = b * s
    hcat = norm_matmul(x, p["norm_mix"], p["w_in"], BF16).reshape(b, s, -1)
    y_conv = conv_module(hcat, p["conv_w"], p["conv_b"], p["conv_norm"])
    cos, sin = _rope_tables(s)
    qh, kh = qk_prep(hcat, cos, sin, p["q_norm"], p["k_norm"])
    y_att = diff_attention(qh, kh, hcat, p["lam"], p["subln"], p["lambda_init"])
    return even_out(x, y_conv.reshape(m, -1), y_att.reshape(m, -1), p["w_out"])


def _odd_layer(x, b, s, p):
    m = b * s
    hcat = norm_matmul(x, p["norm_mix"], p["w_main"], BF16)
    gates = norm_matmul(x, p["norm_mix"], p["w_gates"], F32).reshape(b, s, LANES)
    gates_t = jnp.swapaxes(gates[:, :, :ML_GATES], 1, 2)
    hcat3 = hcat.reshape(b, s, -1)
    h_fwd = mlstm_scan(hcat3, gates, gates_t, p["bias_col"], p["bias_row"], reverse=False)
    h_bwd = mlstm_scan(hcat3, gates, gates_t, p["bias_col"], p["bias_row"], reverse=True)
    return odd_out(x, h_fwd.reshape(m, -1), h_bwd.reshape(m, -1), hcat, p["out_norm"], p["w_out"])


def kernel(x_prompt, x_sample, norm_mix, norm_ffn, ev_w_in, ev_conv_w, ev_conv_b, ev_conv_norm, ev_q_norm, ev_k_norm, ev_lambda, ev_subln, ev_w_out, od_w_in, od_gate_b, od_out_norm, od_w_out, ffn_w_gate, ffn_w_up, ffn_w_down):
    depth = norm_mix.shape[0]
    layers = []
    for i in range(depth):
        j = i // 2
        p = {"norm_mix": norm_mix[i][None], "norm_ffn": norm_ffn[i][None],
             "wg": ffn_w_gate[i].astype(BF16), "wu": ffn_w_up[i].astype(BF16),
             "wd": ffn_w_down[i].astype(BF16)}
        if i % 2 == 0:
            p.update({"w_in": ev_w_in[j].astype(BF16), "conv_w": ev_conv_w[j], "conv_b": ev_conv_b[j][None],
                      "conv_norm": ev_conv_norm[j][None], "q_norm": ev_q_norm[j][None],
                      "k_norm": ev_k_norm[j][None], "lam": ev_lambda[j], "subln": ev_subln[j][None],
                      "w_out": ev_w_out[j].astype(BF16),
                      "lambda_init": 0.8 - 0.6 * math.exp(-0.3 * i)})
        else:
            w_gates = jnp.pad(od_w_in[j][:, ML_MAIN:], ((0, 0), (0, LANES - ML_GATES)))
            p.update({"w_main": od_w_in[j][:, :ML_MAIN].astype(BF16), "w_gates": w_gates.astype(BF16),
                      "bias_col": od_gate_b[j].reshape(ML_GATES, 1),
                      "bias_row": od_gate_b[j].reshape(1, ML_GATES),
                      "out_norm": od_out_norm[j].reshape(1, D_MODEL),
                      "w_out": od_w_out[j].astype(BF16)})
        layers.append(p)

    outs = []
    for x3 in (x_prompt, x_sample):
        b, s, d = x3.shape
        x = x3.reshape(b * s, d)
        for i, p in enumerate(layers):
            x = _even_layer(x, b, s, p) if i % 2 == 0 else _odd_layer(x, b, s, p)
            x = ffn(x, p["norm_ffn"], p["wg"], p["wu"], p["wd"])
        outs.append(x.reshape(b, s, d))
    return tuple(outs)
```

```python
import functools
import math

import jax
import jax.numpy as jnp
from jax import lax
from jax.experimental import pallas as pl
from jax.experimental.pallas import tpu as pltpu

F32 = jnp.float32
BF16 = jnp.bfloat16

D_MODEL = 2048
CONV_CH = D_MODEL // 2
CONV_WIDTH = 31
CONV_PAD = CONV_WIDTH // 2
CONV_HALO = 16
DA_HEAD_DIM = 128
DA_HEADS = D_MODEL // (4 * DA_HEAD_DIM)
DA_QK = DA_HEADS * 2 * DA_HEAD_DIM
ML_HEADS = 4
ML_DV = D_MODEL // ML_HEADS
ML_DK = ML_DV // 2
ML_GATES = 4 * ML_HEADS
ML_MAIN = 2 * ML_HEADS * ML_DK + 2 * ML_HEADS * ML_DV
FFN_HIDDEN = ((8 * D_MODEL // 3 + 255) // 256) * 256
CHUNK = 128
ROPE_THETA = 10000.0
EPS = 1e-6
LANES = 128
LOG2E = math.log2(math.e)
SAFE_LOGIT_LOG2 = 64.0
VMEM_LIMIT = 56 * 1024 * 1024


def _params(semantics):
    return pltpu.CompilerParams(dimension_semantics=semantics, vmem_limit_bytes=VMEM_LIMIT)


def _rms(x, g):
    ms = jnp.mean(x * x, axis=-1, keepdims=True)
    return x * lax.rsqrt(ms + EPS) * g


def _norm_matmul_kernel(x_ref, g_ref, w_ref, o_ref, xn_ref):
    @pl.when(pl.program_id(1) == 0)
    def _():
        xn_ref[...] = _rms(x_ref[...], g_ref[...]).astype(BF16)

    o_ref[...] = jnp.dot(xn_ref[...], w_ref[...], preferred_element_type=F32).astype(o_ref.dtype)


def norm_matmul(x, g, w, out_dtype, tm=1024, tn=1024):
    m, d = x.shape
    n = w.shape[1]
    tn = min(tn, n)
    return pl.pallas_call(
        _norm_matmul_kernel,
        grid=(m // tm, n // tn),
        in_specs=[pl.BlockSpec((tm, d), lambda i, j: (i, 0)),
                  pl.BlockSpec((1, d), lambda i, j: (0, 0)),
                  pl.BlockSpec((d, tn), lambda i, j: (0, j))],
        out_specs=pl.BlockSpec((tm, tn), lambda i, j: (i, j)),
        out_shape=jax.ShapeDtypeStruct((m, n), out_dtype),
        scratch_shapes=[pltpu.VMEM((tm, d), BF16)],
        compiler_params=_params(("parallel", "arbitrary")),
        name="norm_matmul",
    )(x, g, w)


def _conv_kernel(ac_ref, ap_ref, an_ref, gc_ref, gp_ref, gn_ref, w_ref, b_ref, nrm_ref, o_ref, u_ref,
                 *, ts, tr):
    i = pl.program_id(1)
    n = pl.num_programs(1)

    def glu(a_ref, g_ref):
        return a_ref[0].astype(F32) * jax.nn.sigmoid(g_ref[0].astype(F32))

    u_ref[0:CONV_HALO, :] = jnp.where(i > 0, glu(ap_ref, gp_ref), 0.0)
    u_ref[CONV_HALO:CONV_HALO + ts, :] = glu(ac_ref, gc_ref)
    u_ref[CONV_HALO + ts:, :] = jnp.where(i < n - 1, glu(an_ref, gn_ref), 0.0)

    bias = b_ref[...]
    nrm = nrm_ref[...]
    base = CONV_HALO - CONV_PAD
    for r in range(ts // tr):
        acc = jnp.zeros((tr, CONV_CH), F32)
        for k in range(CONV_WIDTH):
            lo = base + r * tr + k
            acc = acc + u_ref[lo:lo + tr, :] * w_ref[k:k + 1, :]
        y = _rms(acc + bias, nrm)
        o_ref[0, r * tr:(r + 1) * tr, :] = (y * jax.nn.sigmoid(y)).astype(o_ref.dtype)


def conv_module(hcat, conv_w, conv_b, conv_norm, ts=256, tr=32):
    b, s, _ = hcat.shape
    nh = ts // CONV_HALO
    last = s // CONV_HALO - 1
    cur = lambda col: pl.BlockSpec((1, ts, CONV_CH), lambda bi, i: (bi, i, col))
    prev = lambda col: pl.BlockSpec((1, CONV_HALO, CONV_CH),
                                    lambda bi, i: (bi, jnp.maximum(i * nh - 1, 0), col))
    nxt = lambda col: pl.BlockSpec((1, CONV_HALO, CONV_CH),
                                   lambda bi, i: (bi, jnp.minimum((i + 1) * nh, last), col))
    vec = lambda rows: pl.BlockSpec((rows, CONV_CH), lambda bi, i: (0, 0))
    return pl.pallas_call(
        functools.partial(_conv_kernel, ts=ts, tr=tr),
        grid=(b, s // ts),
        in_specs=[cur(0), prev(0), nxt(0), cur(1), prev(1), nxt(1), vec(CONV_WIDTH), vec(1), vec(1)],
        out_specs=pl.BlockSpec((1, ts, CONV_CH), lambda bi, i: (bi, i, 0)),
        out_shape=jax.ShapeDtypeStruct((b, s, CONV_CH), BF16),
        scratch_shapes=[pltpu.VMEM((ts + 2 * CONV_HALO, CONV_CH), F32)],
        compiler_params=_params(("parallel", "arbitrary")),
        name="conv_module",
    )(hcat, hcat, hcat, hcat, hcat, hcat, conv_w, conv_b, conv_norm)


def _qk_prep_kernel(q_ref, k_ref, cos_ref, sin_ref, qn_ref, kn_ref, qo_ref, ko_ref, *, scale):
    cos = cos_ref[...]
    sin = sin_ref[...]

    def prep(x_ref, g, o_ref, mul):
        for j in range(DA_QK // DA_HEAD_DIM):
            sl = slice(j * DA_HEAD_DIM, (j + 1) * DA_HEAD_DIM)
            xn = _rms(x_ref[0, :, sl].astype(F32), g)
            rot = pltpu.roll(xn, DA_HEAD_DIM // 2, axis=1)
            o_ref[0, :, sl] = ((xn * cos + rot * sin) * mul).astype(o_ref.dtype)

    prep(q_ref, qn_ref[...], qo_ref, scale)
    prep(k_ref, kn_ref[...], ko_ref, 1.0)


def qk_prep(hcat, cos, sin, q_norm, k_norm, ts=512):
    b, s, _ = hcat.shape
    col = lambda c: pl.BlockSpec((1, ts, DA_QK), lambda bi, i: (bi, i, c))
    tab = pl.BlockSpec((ts, DA_HEAD_DIM), lambda bi, i: (i, 0))
    vec = pl.BlockSpec((1, DA_HEAD_DIM), lambda bi, i: (0, 0))
    out = pl.BlockSpec((1, ts, DA_QK), lambda bi, i: (bi, i, 0))
    return pl.pallas_call(
        functools.partial(_qk_prep_kernel, scale=LOG2E / math.sqrt(DA_HEAD_DIM)),
        grid=(b, s // ts),
        in_specs=[col(2), col(3), tab, tab, vec, vec],
        out_specs=[out, out],
        out_shape=[jax.ShapeDtypeStruct((b, s, DA_QK), BF16)] * 2,
        compiler_params=_params(("parallel", "parallel")),
        name="qk_prep",
    )(hcat, hcat, cos, sin, q_norm, k_norm)


def _attn_kernel(q_ref, k_ref, v_ref, lam_ref, subln_ref, o_ref, m_ref, l_ref, acc_ref,
                 *, lambda_init, stabilize):
    ki = pl.program_id(3)
    d = DA_HEAD_DIM

    @pl.when(ki == 0)
    def _():
        if stabilize:
            m_ref[...] = jnp.full_like(m_ref, -jnp.inf)
        l_ref[...] = jnp.zeros_like(l_ref)
        acc_ref[...] = jnp.zeros_like(acc_ref)

    v = v_ref[0]
    for c in range(2):
        q = q_ref[0, :, c * d:(c + 1) * d]
        k = k_ref[0, :, c * d:(c + 1) * d]
        s = lax.dot_general(q, k, (((1,), (1,)), ((), ())), preferred_element_type=F32)
        if stabilize:
            m_prev = m_ref[c]
            m_new = jnp.maximum(m_prev, jnp.max(s, axis=-1, keepdims=True))
            alpha = jnp.exp2(m_prev - m_new)
            m_ref[c] = m_new
            p = jnp.exp2(s - m_new)
        else:
            p = jnp.exp2(s)
        psum = p[:, 0:LANES]
        for j in range(1, p.shape[1] // LANES):
            psum = psum + p[:, j * LANES:(j + 1) * LANES]
        pv = jnp.dot(p.astype(BF16), v, preferred_element_type=F32)
        if stabilize:
            l_ref[c] = alpha * l_ref[c] + psum
            acc_ref[c] = alpha * acc_ref[c] + pv
        else:
            l_ref[c] += psum
            acc_ref[c] += pv

    @pl.when(ki == pl.num_programs(3) - 1)
    def _():
        lp = lam_ref[...]
        lam = (jnp.exp(jnp.sum(lp[0:1] * lp[1:2], axis=-1, keepdims=True))
               - jnp.exp(jnp.sum(lp[2:3] * lp[3:4], axis=-1, keepdims=True)) + lambda_init)
        l0 = jnp.sum(l_ref[0], axis=-1, keepdims=True)
        l1 = jnp.sum(l_ref[1], axis=-1, keepdims=True)
        o = acc_ref[0] / l0 - lam * (acc_ref[1] / l1)
        o_ref[0] = (_rms(o, subln_ref[...]) * (1.0 - lambda_init)).astype(o_ref.dtype)


def logit_bound_log2(q_norm, k_norm):
    return (LOG2E * math.sqrt(DA_HEAD_DIM)) * jnp.max(jnp.abs(q_norm)) * jnp.max(jnp.abs(k_norm))


def diff_attention(qh, kh, hcat, lam_params, subln, lambda_init, stabilize, tq=1024, tk=1024):
    b, s, _ = qh.shape
    dv = 2 * DA_HEAD_DIM
    v_col0 = (hcat.shape[2] - DA_QK) // dv
    return pl.pallas_call(
        functools.partial(_attn_kernel, lambda_init=lambda_init, stabilize=stabilize),
        grid=(b, DA_HEADS, s // tq, s // tk),
        in_specs=[pl.BlockSpec((1, tq, dv), lambda bi, h, qi, ki: (bi, qi, h)),
                  pl.BlockSpec((1, tk, dv), lambda bi, h, qi, ki: (bi, ki, h)),
                  pl.BlockSpec((1, tk, dv), lambda bi, h, qi, ki: (bi, ki, v_col0 + h)),
                  pl.BlockSpec((4, DA_HEAD_DIM), lambda bi, h, qi, ki: (0, 0)),
                  pl.BlockSpec((1, dv), lambda bi, h, qi, ki: (0, 0))],
        out_specs=pl.BlockSpec((1, tq, dv), lambda bi, h, qi, ki: (bi, qi, h)),
        out_shape=jax.ShapeDtypeStruct((b, s, DA_QK), BF16),
        scratch_shapes=[pltpu.VMEM((2, tq, 1), F32), pltpu.VMEM((2, tq, LANES), F32),
                        pltpu.VMEM((2, tq, dv), F32)],
        compiler_params=_params(("parallel", "parallel", "parallel", "arbitrary")),
        name="diff_attention",
    )(qh, kh, hcat, lam_params, subln)


def _even_out_kernel(x_ref, a_ref, b_ref, w_ref, o_ref):
    ka = a_ref.shape[1]
    y = jnp.dot(a_ref[...], w_ref[0:ka, :], preferred_element_type=F32)
    y = y + jnp.dot(b_ref[...], w_ref[ka:, :], preferred_element_type=F32)
    o_ref[...] = x_ref[...] + y


def even_out(x, y_conv, y_att, w, tm=512):
    m, d = x.shape
    row = lambda width: pl.BlockSpec((tm, width), lambda i: (i, 0))
    return pl.pallas_call(
        _even_out_kernel,
        grid=(m // tm,),
        in_specs=[row(d), row(y_conv.shape[1]), row(y_att.shape[1]),
                  pl.BlockSpec(w.shape, lambda i: (0, 0))],
        out_specs=row(d),
        out_shape=jax.ShapeDtypeStruct((m, d), F32),
        compiler_params=_params(("parallel",)),
        name="even_out",
    )(x, y_conv, y_att, w)


def _odd_out_kernel(x_ref, hf_ref, hb_ref, og_ref, nrm_ref, w_ref, o_ref, z_ref):
    for h in range(ML_HEADS):
        sl = slice(h * ML_DV, (h + 1) * ML_DV)
        hs = hf_ref[:, sl].astype(F32) + hb_ref[:, sl].astype(F32)
        z = _rms(hs, nrm_ref[:, sl]) * jax.nn.sigmoid(og_ref[:, sl].astype(F32))
        z_ref[:, sl] = z.astype(BF16)
    o_ref[...] = x_ref[...] + jnp.dot(z_ref[...], w_ref[...], preferred_element_type=F32)


def odd_out(x, h_fwd, h_bwd, hcat, out_norm, w, tm=512):
    m, d = x.shape
    row = lambda c: pl.BlockSpec((tm, d), lambda i: (i, c))
    return pl.pallas_call(
        _odd_out_kernel,
        grid=(m // tm,),
        in_specs=[row(0), row(0), row(0), row(2),
                  pl.BlockSpec((1, d), lambda i: (0, 0)),
                  pl.BlockSpec(w.shape, lambda i: (0, 0))],
        out_specs=row(0),
        out_shape=jax.ShapeDtypeStruct((m, d), F32),
        scratch_shapes=[pltpu.VMEM((tm, d), BF16)],
        compiler_params=_params(("parallel",)),
        name="odd_out",
    )(x, h_fwd, h_bwd, hcat, out_norm, w)


def _ffn_kernel(x_ref, g_ref, wg_ref, wu_ref, wd_ref, o_ref, xn_ref):
    j = pl.program_id(1)

    @pl.when(j == 0)
    def _():
        x = x_ref[...]
        xn_ref[...] = _rms(x, g_ref[...]).astype(BF16)
        o_ref[...] = x

    xn = xn_ref[...]
    gate = jnp.dot(xn, wg_ref[...], preferred_element_type=F32)
    up = jnp.dot(xn, wu_ref[...], preferred_element_type=F32)
    hid = (gate * jax.nn.sigmoid(gate) * up).astype(BF16)
    o_ref[...] += jnp.dot(hid, wd_ref[...], preferred_element_type=F32)


def ffn(x, g, wg, wu, wd, tm=512, tf=512):
    m, d = x.shape
    f = wg.shape[1]
    return pl.pallas_call(
        _ffn_kernel,
        grid=(m // tm, f // tf),
        in_specs=[pl.BlockSpec((tm, d), lambda i, j: (i, 0)),
                  pl.BlockSpec((1, d), lambda i, j: (0, 0)),
                  pl.BlockSpec((d, tf), lambda i, j: (0, j)),
                  pl.BlockSpec((d, tf), lambda i, j: (0, j)),
                  pl.BlockSpec((tf, d), lambda i, j: (j, 0))],
        out_specs=pl.BlockSpec((tm, d), lambda i, j: (i, 0)),
        out_shape=jax.ShapeDtypeStruct((m, d), F32),
        scratch_shapes=[pltpu.VMEM((tm, d), BF16)],
        compiler_params=_params(("parallel", "arbitrary")),
        name="ffn",
    )(x, g, wg, wu, wd)


def _split3(x):
    x1 = x.astype(BF16)
    r = x - x1.astype(F32)
    x2 = r.astype(BF16)
    x3 = (r - x2.astype(F32)).astype(BF16)
    return x1, x2, x3


def _mlstm_kernel(q_ref, k_ref, v_ref, g_ref, gt_ref, bc_ref, br_ref, h_ref, c_ref, n_ref, m_ref,
                  *, reverse):
    L = CHUNK

    @pl.when(pl.program_id(1) == 0)
    def _():
        c_ref[...] = jnp.zeros_like(c_ref)
        n_ref[...] = jnp.zeros_like(n_ref)
        m_ref[...] = jnp.zeros_like(m_ref)

    t_idx = lax.broadcasted_iota(jnp.int32, (L, L), 0)
    s_idx = lax.broadcasted_iota(jnp.int32, (L, L), 1)
    mask = (s_idx >= t_idx) if reverse else (s_idx <= t_idx)
    mask_t = (s_idx <= t_idx) if reverse else (s_idx >= t_idx)
    tri_col = jnp.where(mask, 1.0, 0.0).astype(BF16)
    tri_row = jnp.where(mask_t, 1.0, 0.0).astype(BF16)
    last = 0 if reverse else L - 1

    g_col = g_ref[0][:, 0:ML_GATES] + br_ref[...]
    g_row = gt_ref[0] + bc_ref[...]
    lf_col = jax.nn.log_sigmoid(g_col)
    lf_row = jax.nn.log_sigmoid(g_row)
    bcum_col = sum(jnp.dot(tri_col, p, preferred_element_type=F32) for p in _split3(lf_col))
    bcum_row = sum(jnp.dot(p, tri_row, preferred_element_type=F32) for p in _split3(lf_row))

    it = 2 if reverse else 0
    for h in range(ML_HEADS):
        ci = it * ML_HEADS + h
        cf = (it + 1) * ML_HEADS + h
        i_col = g_col[:, ci:ci + 1]
        i_row = g_row[ci:ci + 1, :]
        b_col = bcum_col[:, cf:cf + 1]
        b_row = bcum_row[cf:cf + 1, :]
        total = b_col[last:last + 1, :]
        m_prev = m_ref[h]

        q = q_ref[0, :, h * ML_DK:(h + 1) * ML_DK]
        k = k_ref[0, :, h * ML_DK:(h + 1) * ML_DK]
        v = v_ref[0, :, h * ML_DV:(h + 1) * ML_DV]

        dlog = jnp.where(mask, b_col - b_row + i_row, -jnp.inf)
        inter = b_col + m_prev
        mt = jnp.maximum(inter, jnp.max(dlog, axis=-1, keepdims=True))
        dmat = jnp.exp(dlog - mt)
        qk = lax.dot_general(q, k, (((1,), (1,)), ((), ())), preferred_element_type=F32)
        sc = qk * dmat * (1.0 / math.sqrt(ML_DK))
        w_int = jnp.exp(inter - mt)
        qc = jnp.dot(q, c_ref[h].astype(BF16), preferred_element_type=F32)
        num = jnp.dot(sc.astype(BF16), v, preferred_element_type=F32) + w_int * qc
        qn = jnp.sum(q.astype(F32) * n_ref[h], axis=-1, keepdims=True)
        den = jnp.sum(sc, axis=-1, keepdims=True) + w_int * qn
        hout = num / jnp.maximum(jnp.abs(den), jnp.exp(-mt))
        h_ref[0, :, h * ML_DV:(h + 1) * ML_DV] = hout.astype(h_ref.dtype)

        m_new = mt[last:last + 1, :]
        w_s = jnp.exp(total - b_col + i_col - m_new) * (1.0 / math.sqrt(ML_DK))
        decay = jnp.exp(total + m_prev - m_new)
        kw = k.astype(F32) * w_s
        c_ref[h] = decay * c_ref[h] + jnp.dot(kw.T.astype(BF16), v, preferred_element_type=F32)
        n_ref[h] = decay * n_ref[h] + jnp.sum(kw, axis=0, keepdims=True)
        m_ref[h] = m_new


def mlstm_scan(hcat, gates, gates_t, bias_col, bias_row, reverse):
    b, s, _ = hcat.shape
    nc = s // CHUNK
    pos = (lambda c: nc - 1 - c) if reverse else (lambda c: c)
    nq = ML_HEADS * ML_DK
    nv = ML_HEADS * ML_DV
    return pl.pallas_call(
        functools.partial(_mlstm_kernel, reverse=reverse),
        grid=(b, nc),
        in_specs=[pl.BlockSpec((1, CHUNK, nq), lambda bi, c: (bi, pos(c), 0)),
                  pl.BlockSpec((1, CHUNK, nq), lambda bi, c: (bi, pos(c), 1)),
                  pl.BlockSpec((1, CHUNK, nv), lambda bi, c: (bi, pos(c), 1)),
                  pl.BlockSpec((1, CHUNK, LANES), lambda bi, c: (bi, pos(c), 0)),
                  pl.BlockSpec((1, ML_GATES, CHUNK), lambda bi, c: (bi, 0, pos(c))),
                  pl.BlockSpec((ML_GATES, 1), lambda bi, c: (0, 0)),
                  pl.BlockSpec((1, ML_GATES), lambda bi, c: (0, 0))],
        out_specs=pl.BlockSpec((1, CHUNK, nv), lambda bi, c: (bi, pos(c), 0)),
        out_shape=jax.ShapeDtypeStruct((b, s, nv), BF16),
        scratch_shapes=[pltpu.VMEM((ML_HEADS, ML_DK, ML_DV), F32),
                        pltpu.VMEM((ML_HEADS, 1, ML_DK), F32),
                        pltpu.VMEM((ML_HEADS, 1, 1), F32)],
        compiler_params=_params(("parallel", "arbitrary")),
        name="mlstm_bwd" if reverse else "mlstm_fwd",
    )(hcat, hcat, hcat, gates, gates_t, bias_col, bias_row)


def _rope_tables(s):
    d = DA_HEAD_DIM
    inv = 1.0 / (ROPE_THETA ** (jnp.arange(0, d, 2, dtype=F32) / d))
    ang = jnp.arange(s, dtype=F32)[:, None] * inv[None, :]
    cos, sin = jnp.cos(ang), jnp.sin(ang)
    return jnp.concatenate([cos, cos], axis=-1), jnp.concatenate([-sin, sin], axis=-1)


def _even_layer(x, b, s, p):
    m = b * s
    hcat = norm_matmul(x, p["norm_mix"], p["w_in"], BF16).reshape(b, s, -1)
    y_conv = conv_module(hcat, p["conv_w"], p["conv_b"], p["conv_norm"])
    cos, sin = _rope_tables(s)
    qh, kh = qk_prep(hcat, cos, sin, p["q_norm"], p["k_norm"])
    attn = lambda stabilize: functools.partial(diff_attention, lam_params=p["lam"], subln=p["subln"],
                                               lambda_init=p["lambda_init"], stabilize=stabilize)
    y_att = lax.cond(logit_bound_log2(p["q_norm"], p["k_norm"]) < SAFE_LOGIT_LOG2,
                     attn(False), attn(True), qh, kh, hcat)
    return even_out(x, y_conv.reshape(m, -1), y_att.reshape(m, -1), p["w_out"])


def _odd_layer(x, b, s, p):
    m = b * s
    hcat = norm_matmul(x, p["norm_mix"], p["w_main"], BF16)
    gates = norm_matmul(x, p["norm_mix"], p["w_gates"], F32).reshape(b, s, LANES)
    gates_t = jnp.swapaxes(gates[:, :, :ML_GATES], 1, 2)
    hcat3 = hcat.reshape(b, s, -1)
    h_fwd = mlstm_scan(hcat3, gates, gates_t, p["bias_col"], p["bias_row"], reverse=False)
    h_bwd = mlstm_scan(hcat3, gates, gates_t, p["bias_col"], p["bias_row"], reverse=True)
    return odd_out(x, h_fwd.reshape(m, -1), h_bwd.reshape(m, -1), hcat, p["out_norm"], p["w_out"])


def kernel(x_prompt, x_sample, norm_mix, norm_ffn, ev_w_in, ev_conv_w, ev_conv_b, ev_conv_norm, ev_q_norm, ev_k_norm, ev_lambda, ev_subln, ev_w_out, od_w_in, od_gate_b, od_out_norm, od_w_out, ffn_w_gate, ffn_w_up, ffn_w_down):
    depth = norm_mix.shape[0]
    layers = []
    for i in range(depth):
        j = i // 2
        p = {"norm_mix": norm_mix[i][None], "norm_ffn": norm_ffn[i][None],
             "wg": ffn_w_gate[i].astype(BF16), "wu": ffn_w_up[i].astype(BF16),
             "wd": ffn_w_down[i].astype(BF16)}
        if i % 2 == 0:
            p.update({"w_in": ev_w_in[j].astype(BF16), "conv_w": ev_conv_w[j], "conv_b": ev_conv_b[j][None],
                      "conv_norm": ev_conv_norm[j][None], "q_norm": ev_q_norm[j][None],
                      "k_norm": ev_k_norm[j][None], "lam": ev_lambda[j], "subln": ev_subln[j][None],
                      "w_out": ev_w_out[j].astype(BF16),
                      "lambda_init": 0.8 - 0.6 * math.exp(-0.3 * i)})
        else:
            w_gates = jnp.pad(od_w_in[j][:, ML_MAIN:], ((0, 0), (0, LANES - ML_GATES)))
            p.update({"w_main": od_w_in[j][:, :ML_MAIN].astype(BF16), "w_gates": w_gates.astype(BF16),
                      "bias_col": od_gate_b[j].reshape(ML_GATES, 1),
                      "bias_row": od_gate_b[j].reshape(1, ML_GATES),
                      "out_norm": od_out_norm[j].reshape(1, D_MODEL),
                      "w_out": od_w_out[j].astype(BF16)})
        layers.append(p)

    outs = []
    for x3 in (x_prompt, x_sample):
        b, s, d = x3.shape
        x = x3.reshape(b * s, d)
        for i, p in enumerate(layers):
            x = _even_layer(x, b, s, p) if i % 2 == 0 else _odd_layer(x, b, s, p)
            x = ffn(x, p["norm_ffn"], p["wg"], p["wu"], p["wd"])
        outs.append(x.reshape(b, s, d))
    return tuple(outs)
```

```python
import functools
import math

import jax
import jax.numpy as jnp
from jax import lax
from jax.experimental import pallas as pl
from jax.experimental.pallas import tpu as pltpu

F32 = jnp.float32
BF16 = jnp.bfloat16

D_MODEL = 2048
CONV_CH = D_MODEL // 2
CONV_WIDTH = 31
CONV_PAD = CONV_WIDTH // 2
CONV_HALO = 16
DA_HEAD_DIM = 128
DA_HEADS = D_MODEL // (4 * DA_HEAD_DIM)
DA_QK = DA_HEADS * 2 * DA_HEAD_DIM
ML_HEADS = 4
ML_DV = D_MODEL // ML_HEADS
ML_DK = ML_DV // 2
ML_GATES = 4 * ML_HEADS
ML_MAIN = 2 * ML_HEADS * ML_DK + 2 * ML_HEADS * ML_DV
FFN_HIDDEN = ((8 * D_MODEL // 3 + 255) // 256) * 256
CHUNK = 128
ROPE_THETA = 10000.0
EPS = 1e-6
LANES = 128
SUBLANES = 8
LOG2E = math.log2(math.e)
SAFE_LOGIT_LOG2 = 64.0
VMEM_LIMIT = 56 * 1024 * 1024


def _params(semantics):
    return pltpu.CompilerParams(dimension_semantics=semantics, vmem_limit_bytes=VMEM_LIMIT)


def _rms(x, g):
    ms = jnp.mean(x * x, axis=-1, keepdims=True)
    return x * lax.rsqrt(ms + EPS) * g


def _norm_matmul_kernel(x_ref, g_ref, w_ref, o_ref, xn_ref, w_side_ref=None, o_side_ref=None):
    @pl.when(pl.program_id(1) == 0)
    def _():
        xn = _rms(x_ref[...], g_ref[...]).astype(BF16)
        xn_ref[...] = xn
        if w_side_ref is not None:
            o_side_ref[...] = jnp.dot(xn, w_side_ref[...], preferred_element_type=F32)

    o_ref[...] = jnp.dot(xn_ref[...], w_ref[...], preferred_element_type=F32).astype(o_ref.dtype)


def _norm_matmul_side_kernel(x_ref, g_ref, w_ref, w_side_ref, o_ref, o_side_ref, xn_ref):
    _norm_matmul_kernel(x_ref, g_ref, w_ref, o_ref, xn_ref, w_side_ref, o_side_ref)


def norm_matmul(x, g, w, w_side=None, tm=1024, tn=1024):
    m, d = x.shape
    n = w.shape[1]
    tm, tn = min(tm, m), min(tn, n)
    assert m % tm == 0 and n % tn == 0, (m, n, tm, tn)
    in_specs = [pl.BlockSpec((tm, d), lambda i, j: (i, 0)),
                pl.BlockSpec((1, d), lambda i, j: (0, 0)),
                pl.BlockSpec((d, tn), lambda i, j: (0, j))]
    out_specs = [pl.BlockSpec((tm, tn), lambda i, j: (i, j))]
    out_shape = [jax.ShapeDtypeStruct((m, n), BF16)]
    operands = [x, g, w]
    if w_side is not None:
        ns = w_side.shape[1]
        in_specs.append(pl.BlockSpec((d, ns), lambda i, j: (0, 0)))
        out_specs.append(pl.BlockSpec((tm, ns), lambda i, j: (i, 0)))
        out_shape.append(jax.ShapeDtypeStruct((m, ns), F32))
        operands.append(w_side)
    outs = pl.pallas_call(
        _norm_matmul_kernel if w_side is None else _norm_matmul_side_kernel,
        grid=(m // tm, n // tn),
        in_specs=in_specs,
        out_specs=out_specs,
        out_shape=out_shape,
        scratch_shapes=[pltpu.VMEM((tm, d), BF16)],
        compiler_params=_params(("parallel", "arbitrary")),
        name="norm_matmul",
    )(*operands)
    return outs[0] if w_side is None else outs


def _conv_kernel(ac_ref, ap_ref, an_ref, gc_ref, gp_ref, gn_ref, w_ref, b_ref, nrm_ref, o_ref, u_ref,
                 *, ts, tr):
    i = pl.program_id(1)
    n = pl.num_programs(1)

    def glu(a_ref, g_ref):
        return a_ref[0].astype(F32) * jax.nn.sigmoid(g_ref[0].astype(F32))

    u_ref[0, 0:CONV_HALO, :] = jnp.where(i > 0, glu(ap_ref, gp_ref), 0.0)
    u_ref[0, CONV_HALO:CONV_HALO + ts, :] = glu(ac_ref, gc_ref)
    u_ref[0, CONV_HALO + ts:, :] = jnp.where(i < n - 1, glu(an_ref, gn_ref), 0.0)
    rows = ts + 2 * CONV_HALO - SUBLANES
    for sh in range(1, SUBLANES):
        u_ref[sh, 0:rows, :] = u_ref[0, sh:sh + rows, :]

    bias = b_ref[...]
    nrm = nrm_ref[...]
    base = CONV_HALO - CONV_PAD
    for r in range(ts // tr):
        acc = jnp.zeros((tr, CONV_CH), F32)
        for k in range(CONV_WIDTH):
            sh, lo = (base + k) % SUBLANES, r * tr + (base + k) // SUBLANES * SUBLANES
            w_tap = w_ref[k * SUBLANES:(k + 1) * SUBLANES, :]
            acc = acc + u_ref[sh, lo:lo + tr, :] * jnp.concatenate([w_tap] * (tr // SUBLANES), axis=0)
        y = _rms(acc + bias, nrm)
        o_ref[0, r * tr:(r + 1) * tr, :] = (y * jax.nn.sigmoid(y)).astype(o_ref.dtype)


def conv_module(hcat, conv_w, conv_b, conv_norm, ts=256, tr=32):
    b, s, _ = hcat.shape
    assert s % ts == 0 and ts % tr == 0 and ts % CONV_HALO == 0, (s, ts, tr)
    nh = ts // CONV_HALO
    last = s // CONV_HALO - 1
    cur = lambda col: pl.BlockSpec((1, ts, CONV_CH), lambda bi, i: (bi, i, col))
    prev = lambda col: pl.BlockSpec((1, CONV_HALO, CONV_CH),
                                    lambda bi, i: (bi, jnp.maximum(i * nh - 1, 0), col))
    nxt = lambda col: pl.BlockSpec((1, CONV_HALO, CONV_CH),
                                   lambda bi, i: (bi, jnp.minimum((i + 1) * nh, last), col))
    vec = lambda rows: pl.BlockSpec((rows, CONV_CH), lambda bi, i: (0, 0))
    return pl.pallas_call(
        functools.partial(_conv_kernel, ts=ts, tr=tr),
        grid=(b, s // ts),
        in_specs=[cur(0), prev(0), nxt(0), cur(1), prev(1), nxt(1), vec(CONV_WIDTH * SUBLANES), vec(1), vec(1)],
        out_specs=pl.BlockSpec((1, ts, CONV_CH), lambda bi, i: (bi, i, 0)),
        out_shape=jax.ShapeDtypeStruct((b, s, CONV_CH), BF16),
        scratch_shapes=[pltpu.VMEM((SUBLANES, ts + 2 * CONV_HALO, CONV_CH), F32)],
        compiler_params=_params(("parallel", "arbitrary")),
        name="conv_module",
    )(hcat, hcat, hcat, hcat, hcat, hcat, jnp.repeat(conv_w, SUBLANES, axis=0), conv_b, conv_norm)


def _qk_prep_kernel(q_ref, k_ref, cos_ref, sin_ref, qn_ref, kn_ref, qo_ref, ko_ref, *, scale):
    cos = cos_ref[...]
    sin = sin_ref[...]

    def prep(x_ref, g, o_ref, mul):
        for j in range(DA_QK // DA_HEAD_DIM):
            sl = slice(j * DA_HEAD_DIM, (j + 1) * DA_HEAD_DIM)
            xn = _rms(x_ref[0, :, sl].astype(F32), g)
            rot = pltpu.roll(xn, DA_HEAD_DIM // 2, axis=1)
            o_ref[0, :, sl] = ((xn * cos + rot * sin) * mul).astype(o_ref.dtype)

    prep(q_ref, qn_ref[...], qo_ref, scale)
    prep(k_ref, kn_ref[...], ko_ref, 1.0)


def qk_prep(hcat, cos, sin, q_norm, k_norm, ts=512):
    b, s, _ = hcat.shape
    assert s % ts == 0, (s, ts)
    col = lambda c: pl.BlockSpec((1, ts, DA_QK), lambda bi, i: (bi, i, c))
    tab = pl.BlockSpec((ts, DA_HEAD_DIM), lambda bi, i: (i, 0))
    vec = pl.BlockSpec((1, DA_HEAD_DIM), lambda bi, i: (0, 0))
    out = pl.BlockSpec((1, ts, DA_QK), lambda bi, i: (bi, i, 0))
    return pl.pallas_call(
        functools.partial(_qk_prep_kernel, scale=LOG2E / math.sqrt(DA_HEAD_DIM)),
        grid=(b, s // ts),
        in_specs=[col(2), col(3), tab, tab, vec, vec],
        out_specs=[out, out],
        out_shape=[jax.ShapeDtypeStruct((b, s, DA_QK), BF16)] * 2,
        compiler_params=_params(("parallel", "parallel")),
        name="qk_prep",
    )(hcat, hcat, cos, sin, q_norm, k_norm)


def _attn_kernel(q_ref, k_ref, v_ref, lam_ref, subln_ref, o_ref, m_ref, l_ref, acc_ref,
                 *, lambda_init, stabilize):
    ki = pl.program_id(3)
    d = DA_HEAD_DIM

    @pl.when(ki == 0)
    def _():
        if stabilize:
            m_ref[...] = jnp.full_like(m_ref, -jnp.inf)
        l_ref[...] = jnp.zeros_like(l_ref)
        acc_ref[...] = jnp.zeros_like(acc_ref)

    v = v_ref[0]
    for c in range(2):
        q = q_ref[0, :, c * d:(c + 1) * d]
        k = k_ref[0, :, c * d:(c + 1) * d]
        s = lax.dot_general(q, k, (((1,), (1,)), ((), ())), preferred_element_type=F32)
        if stabilize:
            m_prev = m_ref[c]
            m_new = jnp.maximum(m_prev, jnp.max(s, axis=-1, keepdims=True))
            alpha = jnp.exp2(m_prev - m_new)
            m_ref[c] = m_new
            p = jnp.exp2(s - m_new)
        else:
            p = jnp.exp2(s)
        psum = p[:, 0:LANES]
        for j in range(1, p.shape[1] // LANES):
            psum = psum + p[:, j * LANES:(j + 1) * LANES]
        pv = jnp.dot(p.astype(BF16), v, preferred_element_type=F32)
        if stabilize:
            l_ref[c] = alpha * l_ref[c] + psum
            acc_ref[c] = alpha * acc_ref[c] + pv
        else:
            l_ref[c] += psum
            acc_ref[c] += pv

    @pl.when(ki == pl.num_programs(3) - 1)
    def _():
        lp = lam_ref[...]
        lam = (jnp.exp(jnp.sum(lp[0:1] * lp[1:2], axis=-1, keepdims=True))
               - jnp.exp(jnp.sum(lp[2:3] * lp[3:4], axis=-1, keepdims=True)) + lambda_init)
        l0 = jnp.sum(l_ref[0], axis=-1, keepdims=True)
        l1 = jnp.sum(l_ref[1], axis=-1, keepdims=True)
        o = acc_ref[0] / l0 - lam * (acc_ref[1] / l1)
        o_ref[0] = (_rms(o, subln_ref[...]) * (1.0 - lambda_init)).astype(o_ref.dtype)


def logit_bound_log2(q_norm, k_norm):
    return (LOG2E * math.sqrt(DA_HEAD_DIM)) * jnp.max(jnp.abs(q_norm)) * jnp.max(jnp.abs(k_norm))


def diff_attention(qh, kh, hcat, lam_params, subln, lambda_init, stabilize, tq=1024, tk=2048):
    b, s, _ = qh.shape
    tq, tk = min(tq, s), min(tk, s)
    assert s % tq == 0 and s % tk == 0, (s, tq, tk)
    dv = 2 * DA_HEAD_DIM
    v_col0 = (hcat.shape[2] - DA_QK) // dv
    return pl.pallas_call(
        functools.partial(_attn_kernel, lambda_init=lambda_init, stabilize=stabilize),
        grid=(b, DA_HEADS, s // tq, s // tk),
        in_specs=[pl.BlockSpec((1, tq, dv), lambda bi, h, qi, ki: (bi, qi, h)),
                  pl.BlockSpec((1, tk, dv), lambda bi, h, qi, ki: (bi, ki, h)),
                  pl.BlockSpec((1, tk, dv), lambda bi, h, qi, ki: (bi, ki, v_col0 + h)),
                  pl.BlockSpec((4, DA_HEAD_DIM), lambda bi, h, qi, ki: (0, 0)),
                  pl.BlockSpec((1, dv), lambda bi, h, qi, ki: (0, 0))],
        out_specs=pl.BlockSpec((1, tq, dv), lambda bi, h, qi, ki: (bi, qi, h)),
        out_shape=jax.ShapeDtypeStruct((b, s, DA_QK), BF16),
        scratch_shapes=[pltpu.VMEM((2, tq, 1), F32), pltpu.VMEM((2, tq, LANES), F32),
                        pltpu.VMEM((2, tq, dv), F32)],
        compiler_params=_params(("parallel", "parallel", "parallel", "arbitrary")),
        name="diff_attention",
    )(qh, kh, hcat, lam_params, subln)


def _even_out_kernel(x_ref, a_ref, b_ref, w_ref, o_ref):
    ka = a_ref.shape[1]
    y = jnp.dot(a_ref[...], w_ref[0:ka, :], preferred_element_type=F32)
    y = y + jnp.dot(b_ref[...], w_ref[ka:, :], preferred_element_type=F32)
    o_ref[...] = x_ref[...] + y


def even_out(x, y_conv, y_att, w, tm=512):
    m, d = x.shape
    assert m % tm == 0, (m, tm)
    row = lambda width: pl.BlockSpec((tm, width), lambda i: (i, 0))
    return pl.pallas_call(
        _even_out_kernel,
        grid=(m // tm,),
        in_specs=[row(d), row(y_conv.shape[1]), row(y_att.shape[1]),
                  pl.BlockSpec(w.shape, lambda i: (0, 0))],
        out_specs=row(d),
        out_shape=jax.ShapeDtypeStruct((m, d), F32),
        compiler_params=_params(("parallel",)),
        name="even_out",
    )(x, y_conv, y_att, w)


def _odd_out_kernel(x_ref, hf_ref, hb_ref, og_ref, nrm_ref, w_ref, o_ref, z_ref):
    for h in range(ML_HEADS):
        sl = slice(h * ML_DV, (h + 1) * ML_DV)
        hs = hf_ref[:, sl].astype(F32) + hb_ref[:, sl].astype(F32)
        z = _rms(hs, nrm_ref[:, sl]) * jax.nn.sigmoid(og_ref[:, sl].astype(F32))
        z_ref[:, sl] = z.astype(BF16)
    o_ref[...] = x_ref[...] + jnp.dot(z_ref[...], w_ref[...], preferred_element_type=F32)


def odd_out(x, h_fwd, h_bwd, hcat, out_norm, w, tm=512):
    m, d = x.shape
    assert m % tm == 0, (m, tm)
    row = lambda c: pl.BlockSpec((tm, d), lambda i: (i, c))
    return pl.pallas_call(
        _odd_out_kernel,
        grid=(m // tm,),
        in_specs=[row(0), row(0), row(0), row(2),
                  pl.BlockSpec((1, d), lambda i: (0, 0)),
                  pl.BlockSpec(w.shape, lambda i: (0, 0))],
        out_specs=row(0),
        out_shape=jax.ShapeDtypeStruct((m, d), F32),
        scratch_shapes=[pltpu.VMEM((tm, d), BF16)],
        compiler_params=_params(("parallel",)),
        name="odd_out",
    )(x, h_fwd, h_bwd, hcat, out_norm, w)


def _ffn_kernel(x_ref, g_ref, wg_ref, wu_ref, wd_ref, o_ref, xn_ref):
    j = pl.program_id(1)

    @pl.when(j == 0)
    def _():
        x = x_ref[...]
        xn_ref[...] = _rms(x, g_ref[...]).astype(BF16)
        o_ref[...] = x

    xn = xn_ref[...]
    gate = jnp.dot(xn, wg_ref[...], preferred_element_type=F32)
    up = jnp.dot(xn, wu_ref[...], preferred_element_type=F32)
    hid = (gate * jax.nn.sigmoid(gate) * up).astype(BF16)
    o_ref[...] += jnp.dot(hid, wd_ref[...], preferred_element_type=F32)


def ffn(x, g, wg, wu, wd, tm=512, tf=512):
    m, d = x.shape
    f = wg.shape[1]
    assert m % tm == 0 and f % tf == 0, (m, f, tm, tf)
    return pl.pallas_call(
        _ffn_kernel,
        grid=(m // tm, f // tf),
        in_specs=[pl.BlockSpec((tm, d), lambda i, j: (i, 0)),
                  pl.BlockSpec((1, d), lambda i, j: (0, 0)),
                  pl.BlockSpec((d, tf), lambda i, j: (0, j)),
                  pl.BlockSpec((d, tf), lambda i, j: (0, j)),
                  pl.BlockSpec((tf, d), lambda i, j: (j, 0))],
        out_specs=pl.BlockSpec((tm, d), lambda i, j: (i, 0)),
        out_shape=jax.ShapeDtypeStruct((m, d), F32),
        scratch_shapes=[pltpu.VMEM((tm, d), BF16)],
        compiler_params=_params(("parallel", "arbitrary")),
        name="ffn",
    )(x, g, wg, wu, wd)


def _split3(x):
    x1 = x.astype(BF16)
    r = x - x1.astype(F32)
    x2 = r.astype(BF16)
    x3 = (r - x2.astype(F32)).astype(BF16)
    return x1, x2, x3


def _mlstm_kernel(qf_ref, kf_ref, vf_ref, gf_ref, gtf_ref, qb_ref, kb_ref, vb_ref, gb_ref, gtb_ref,
                  bc_ref, br_ref, hf_ref, hb_ref, c_ref, n_ref, m_ref):
    @pl.when(pl.program_id(1) == 0)
    def _():
        c_ref[...] = jnp.zeros_like(c_ref)
        n_ref[...] = jnp.zeros_like(n_ref)
        m_ref[...] = jnp.zeros_like(m_ref)

    _mlstm_direction(qf_ref, kf_ref, vf_ref, gf_ref, gtf_ref, bc_ref, br_ref, hf_ref,
                     c_ref.at[0], n_ref.at[0], m_ref.at[0], reverse=False)
    _mlstm_direction(qb_ref, kb_ref, vb_ref, gb_ref, gtb_ref, bc_ref, br_ref, hb_ref,
                     c_ref.at[1], n_ref.at[1], m_ref.at[1], reverse=True)


def _mlstm_direction(q_ref, k_ref, v_ref, g_ref, gt_ref, bc_ref, br_ref, h_ref, c_ref, n_ref, m_ref,
                     *, reverse):
    L = CHUNK
    t_idx = lax.broadcasted_iota(jnp.int32, (L, L), 0)
    s_idx = lax.broadcasted_iota(jnp.int32, (L, L), 1)
    mask = (s_idx >= t_idx) if reverse else (s_idx <= t_idx)
    mask_t = (s_idx <= t_idx) if reverse else (s_idx >= t_idx)
    tri_col = jnp.where(mask, 1.0, 0.0).astype(BF16)
    tri_row = jnp.where(mask_t, 1.0, 0.0).astype(BF16)
    last = 0 if reverse else L - 1

    g_col = g_ref[0][:, 0:ML_GATES] + br_ref[...]
    g_row = gt_ref[0] + bc_ref[...]
    lf_col = jax.nn.log_sigmoid(g_col)
    lf_row = jax.nn.log_sigmoid(g_row)
    bcum_col = sum(jnp.dot(tri_col, p, preferred_element_type=F32) for p in _split3(lf_col))
    bcum_row = sum(jnp.dot(p, tri_row, preferred_element_type=F32) for p in _split3(lf_row))

    it = 2 if reverse else 0
    for h in range(ML_HEADS):
        ci = it * ML_HEADS + h
        cf = (it + 1) * ML_HEADS + h
        i_col = g_col[:, ci:ci + 1]
        i_row = g_row[ci:ci + 1, :]
        b_col = bcum_col[:, cf:cf + 1]
        b_row = bcum_row[cf:cf + 1, :]
        total = b_col[last:last + 1, :]
        m_prev = m_ref[h]

        q = q_ref[0, :, h * ML_DK:(h + 1) * ML_DK]
        k = k_ref[0, :, h * ML_DK:(h + 1) * ML_DK]
        v = v_ref[0, :, h * ML_DV:(h + 1) * ML_DV]

        dlog = jnp.where(mask, b_col - b_row + i_row, -jnp.inf)
        inter = b_col + m_prev
        mt = jnp.maximum(inter, jnp.max(dlog, axis=-1, keepdims=True))
        dmat = jnp.exp(dlog - mt)
        qk = lax.dot_general(q, k, (((1,), (1,)), ((), ())), preferred_element_type=F32)
        sc = qk * dmat * (1.0 / math.sqrt(ML_DK))
        w_int = jnp.exp(inter - mt)
        qc = jnp.dot(q, c_ref[h].astype(BF16), preferred_element_type=F32)
        num = jnp.dot(sc.astype(BF16), v, preferred_element_type=F32) + w_int * qc
        qn = jnp.sum(q.astype(F32) * n_ref[h], axis=-1, keepdims=True)
        den = jnp.sum(sc, axis=-1, keepdims=True) + w_int * qn
        hout = num / jnp.maximum(jnp.abs(den), jnp.exp(-mt))
        h_ref[0, :, h * ML_DV:(h + 1) * ML_DV] = hout.astype(h_ref.dtype)

        m_new = mt[last:last + 1, :]
        w_s = jnp.exp(total - b_col + i_col - m_new) * (1.0 / math.sqrt(ML_DK))
        decay = jnp.exp(total + m_prev - m_new)
        kw = k.astype(F32) * w_s
        c_ref[h] = decay * c_ref[h] + jnp.dot(kw.T.astype(BF16), v, preferred_element_type=F32)
        n_ref[h] = decay * n_ref[h] + jnp.sum(kw, axis=0, keepdims=True)
        m_ref[h] = m_new


def mlstm_scan(hcat, gates, gates_t, bias_col, bias_row):
    b, s, _ = hcat.shape
    assert s % CHUNK == 0, (s, CHUNK)
    nc = s // CHUNK
    nq = ML_HEADS * ML_DK
    nv = ML_HEADS * ML_DV
    fwd = lambda c: c
    bwd = lambda c: nc - 1 - c

    def direction_specs(pos):
        return [pl.BlockSpec((1, CHUNK, nq), lambda bi, c: (bi, pos(c), 0)),
                pl.BlockSpec((1, CHUNK, nq), lambda bi, c: (bi, pos(c), 1)),
                pl.BlockSpec((1, CHUNK, nv), lambda bi, c: (bi, pos(c), 1)),
                pl.BlockSpec((1, CHUNK, LANES), lambda bi, c: (bi, pos(c), 0)),
                pl.BlockSpec((1, ML_GATES, CHUNK), lambda bi, c: (bi, 0, pos(c)))]

    out_spec = lambda pos: pl.BlockSpec((1, CHUNK, nv), lambda bi, c: (bi, pos(c), 0))
    operands = (hcat, hcat, hcat, gates, gates_t)
    return pl.pallas_call(
        _mlstm_kernel,
        grid=(b, nc),
        in_specs=direction_specs(fwd) + direction_specs(bwd)
        + [pl.BlockSpec((ML_GATES, 1), lambda bi, c: (0, 0)),
           pl.BlockSpec((1, ML_GATES), lambda bi, c: (0, 0))],
        out_specs=[out_spec(fwd), out_spec(bwd)],
        out_shape=[jax.ShapeDtypeStruct((b, s, nv), BF16)] * 2,
        scratch_shapes=[pltpu.VMEM((2, ML_HEADS, ML_DK, ML_DV), F32),
                        pltpu.VMEM((2, ML_HEADS, 1, ML_DK), F32),
                        pltpu.VMEM((2, ML_HEADS, 1, 1), F32)],
        compiler_params=_params(("parallel", "arbitrary")),
        name="mlstm",
    )(*operands, *operands, bias_col, bias_row)


def _rope_tables(s):
    d = DA_HEAD_DIM
    inv = 1.0 / (ROPE_THETA ** (jnp.arange(0, d, 2, dtype=F32) / d))
    ang = jnp.arange(s, dtype=F32)[:, None] * inv[None, :]
    cos, sin = jnp.cos(ang), jnp.sin(ang)
    return jnp.concatenate([cos, cos], axis=-1), jnp.concatenate([-sin, sin], axis=-1)


def _even_layer(x, b, s, p):
    m = b * s
    hcat = norm_matmul(x, p["norm_mix"], p["w_in"]).reshape(b, s, -1)
    y_conv = conv_module(hcat, p["conv_w"], p["conv_b"], p["conv_norm"])
    cos, sin = _rope_tables(s)
    qh, kh = qk_prep(hcat, cos, sin, p["q_norm"], p["k_norm"])
    attn = lambda stabilize: functools.partial(diff_attention, lam_params=p["lam"], subln=p["subln"],
                                               lambda_init=p["lambda_init"], stabilize=stabilize)
    y_att = lax.cond(logit_bound_log2(p["q_norm"], p["k_norm"]) < SAFE_LOGIT_LOG2,
                     attn(False), attn(True), qh, kh, hcat)
    return even_out(x, y_conv.reshape(m, -1), y_att.reshape(m, -1), p["w_out"])


def _odd_layer(x, b, s, p):
    m = b * s
    hcat, gates = norm_matmul(x, p["norm_mix"], p["w_main"], w_side=p["w_gates"])
    gates = gates.reshape(b, s, LANES)
    gates_t = jnp.swapaxes(gates[:, :, :ML_GATES], 1, 2)
    hcat3 = hcat.reshape(b, s, -1)
    h_fwd, h_bwd = mlstm_scan(hcat3, gates, gates_t, p["bias_col"], p["bias_row"])
    return odd_out(x, h_fwd.reshape(m, -1), h_bwd.reshape(m, -1), hcat, p["out_norm"], p["w_out"])


def kernel(x_prompt, x_sample, norm_mix, norm_ffn, ev_w_in, ev_conv_w, ev_conv_b, ev_conv_norm, ev_q_norm, ev_k_norm, ev_lambda, ev_subln, ev_w_out, od_w_in, od_gate_b, od_out_norm, od_w_out, ffn_w_gate, ffn_w_up, ffn_w_down):
    depth = norm_mix.shape[0]
    layers = []
    for i in range(depth):
        j = i // 2
        p = {"norm_mix": norm_mix[i][None], "norm_ffn": norm_ffn[i][None],
             "wg": ffn_w_gate[i].astype(BF16), "wu": ffn_w_up[i].astype(BF16),
             "wd": ffn_w_down[i].astype(BF16)}
        if i % 2 == 0:
            p.update({"w_in": ev_w_in[j].astype(BF16), "conv_w": ev_conv_w[j], "conv_b": ev_conv_b[j][None],
                      "conv_norm": ev_conv_norm[j][None], "q_norm": ev_q_norm[j][None],
                      "k_norm": ev_k_norm[j][None], "lam": ev_lambda[j], "subln": ev_subln[j][None],
                      "w_out": ev_w_out[j].astype(BF16),
                      "lambda_init": 0.8 - 0.6 * math.exp(-0.3 * i)})
        else:
            w_gates = jnp.pad(od_w_in[j][:, ML_MAIN:], ((0, 0), (0, LANES - ML_GATES)))
            p.update({"w_main": od_w_in[j][:, :ML_MAIN].astype(BF16), "w_gates": w_gates.astype(BF16),
                      "bias_col": od_gate_b[j].reshape(ML_GATES, 1),
                      "bias_row": od_gate_b[j].reshape(1, ML_GATES),
                      "out_norm": od_out_norm[j].reshape(1, D_MODEL),
                      "w_out": od_w_out[j].astype(BF16)})
        layers.append(p)

    outs = []
    for x3 in (x_prompt, x_sample):
        b, s, d = x3.shape
        x = x3.reshape(b * s, d)
        for i, p in enumerate(layers):
            x = _even_layer(x, b, s, p) if i % 2 == 0 else _odd_layer(x, b, s, p)
            x = ffn(x, p["norm_ffn"], p["wg"], p["wu"], p["wd"])
        outs.append(x.reshape(b, s, d))
    return tuple(outs)
```

```python
import functools
import math

import jax
import jax.numpy as jnp
from jax import lax
from jax.experimental import pallas as pl
from jax.experimental.pallas import tpu as pltpu

F32 = jnp.float32
BF16 = jnp.bfloat16

D_MODEL = 2048
CONV_CH = D_MODEL // 2
CONV_WIDTH = 31
CONV_PAD = CONV_WIDTH // 2
CONV_HALO = 16
DA_HEAD_DIM = 128
DA_HEADS = D_MODEL // (4 * DA_HEAD_DIM)
DA_QK = DA_HEADS * 2 * DA_HEAD_DIM
ML_HEADS = 4
ML_DV = D_MODEL // ML_HEADS
ML_DK = ML_DV // 2
ML_GATES = 4 * ML_HEADS
ML_MAIN = 2 * ML_HEADS * ML_DK + 2 * ML_HEADS * ML_DV
FFN_HIDDEN = ((8 * D_MODEL // 3 + 255) // 256) * 256
CHUNK = 128
ROPE_THETA = 10000.0
EPS = 1e-6
LANES = 128
SUBLANES = 8
LOG2E = math.log2(math.e)
SAFE_LOGIT_LOG2 = 64.0
VMEM_LIMIT = 56 * 1024 * 1024


def _params(semantics):
    return pltpu.CompilerParams(dimension_semantics=semantics, vmem_limit_bytes=VMEM_LIMIT)


def _rms(x, g):
    ms = jnp.mean(x * x, axis=-1, keepdims=True)
    return x * lax.rsqrt(ms + EPS) * g


def _norm_matmul_kernel(x_ref, g_ref, w_ref, o_ref, xn_ref, w_side_ref=None, o_side_ref=None):
    @pl.when(pl.program_id(1) == 0)
    def _():
        xn = _rms(x_ref[...], g_ref[...]).astype(BF16)
        xn_ref[...] = xn
        if w_side_ref is not None:
            o_side_ref[...] = jnp.dot(xn, w_side_ref[...], preferred_element_type=F32)

    o_ref[...] = jnp.dot(xn_ref[...], w_ref[...], preferred_element_type=F32).astype(o_ref.dtype)


def _norm_matmul_side_kernel(x_ref, g_ref, w_ref, w_side_ref, o_ref, o_side_ref, xn_ref):
    _norm_matmul_kernel(x_ref, g_ref, w_ref, o_ref, xn_ref, w_side_ref, o_side_ref)


def norm_matmul(x, g, w, w_side=None, tm=1024, tn=1024):
    m, d = x.shape
    n = w.shape[1]
    tm, tn = min(tm, m), min(tn, n)
    assert m % tm == 0 and n % tn == 0, (m, n, tm, tn)
    in_specs = [pl.BlockSpec((tm, d), lambda i, j: (i, 0)),
                pl.BlockSpec((1, d), lambda i, j: (0, 0)),
                pl.BlockSpec((d, tn), lambda i, j: (0, j))]
    out_specs = [pl.BlockSpec((tm, tn), lambda i, j: (i, j))]
    out_shape = [jax.ShapeDtypeStruct((m, n), BF16)]
    operands = [x, g, w]
    if w_side is not None:
        ns = w_side.shape[1]
        in_specs.append(pl.BlockSpec((d, ns), lambda i, j: (0, 0)))
        out_specs.append(pl.BlockSpec((tm, ns), lambda i, j: (i, 0)))
        out_shape.append(jax.ShapeDtypeStruct((m, ns), F32))
        operands.append(w_side)
    outs = pl.pallas_call(
        _norm_matmul_kernel if w_side is None else _norm_matmul_side_kernel,
        grid=(m // tm, n // tn),
        in_specs=in_specs,
        out_specs=out_specs,
        out_shape=out_shape,
        scratch_shapes=[pltpu.VMEM((tm, d), BF16)],
        compiler_params=_params(("parallel", "arbitrary")),
        name="norm_matmul",
    )(*operands)
    return outs[0] if w_side is None else outs


def _conv_kernel(ac_ref, ap_ref, an_ref, gc_ref, gp_ref, gn_ref, w_ref, b_ref, nrm_ref, o_ref, u_ref,
                 *, ts, tr):
    i = pl.program_id(1)
    n = pl.num_programs(1)

    def glu(a_ref, g_ref):
        return a_ref[0].astype(F32) * jax.nn.sigmoid(g_ref[0].astype(F32))

    u_ref[0, 0:CONV_HALO, :] = jnp.where(i > 0, glu(ap_ref, gp_ref), 0.0)
    u_ref[0, CONV_HALO:CONV_HALO + ts, :] = glu(ac_ref, gc_ref)
    u_ref[0, CONV_HALO + ts:, :] = jnp.where(i < n - 1, glu(an_ref, gn_ref), 0.0)
    rows = ts + 2 * CONV_HALO - SUBLANES
    for sh in range(1, SUBLANES):
        u_ref[sh, 0:rows, :] = u_ref[0, sh:sh + rows, :]

    bias = b_ref[...]
    nrm = nrm_ref[...]
    base = CONV_HALO - CONV_PAD
    for r in range(ts // tr):
        acc = jnp.zeros((tr, CONV_CH), F32)
        for k in range(CONV_WIDTH):
            sh, lo = (base + k) % SUBLANES, r * tr + (base + k) // SUBLANES * SUBLANES
            w_tap = w_ref[k * SUBLANES:(k + 1) * SUBLANES, :]
            acc = acc + u_ref[sh, lo:lo + tr, :] * jnp.concatenate([w_tap] * (tr // SUBLANES), axis=0)
        y = _rms(acc + bias, nrm)
        o_ref[0, r * tr:(r + 1) * tr, :] = (y * jax.nn.sigmoid(y)).astype(o_ref.dtype)


def conv_module(hcat, conv_w, conv_b, conv_norm, ts=256, tr=32):
    b, s, _ = hcat.shape
    assert s % ts == 0 and ts % tr == 0 and ts % CONV_HALO == 0, (s, ts, tr)
    nh = ts // CONV_HALO
    last = s // CONV_HALO - 1
    cur = lambda col: pl.BlockSpec((1, ts, CONV_CH), lambda bi, i: (bi, i, col))
    prev = lambda col: pl.BlockSpec((1, CONV_HALO, CONV_CH),
                                    lambda bi, i: (bi, jnp.maximum(i * nh - 1, 0), col))
    nxt = lambda col: pl.BlockSpec((1, CONV_HALO, CONV_CH),
                                   lambda bi, i: (bi, jnp.minimum((i + 1) * nh, last), col))
    vec = lambda rows: pl.BlockSpec((rows, CONV_CH), lambda bi, i: (0, 0))
    return pl.pallas_call(
        functools.partial(_conv_kernel, ts=ts, tr=tr),
        grid=(b, s // ts),
        in_specs=[cur(0), prev(0), nxt(0), cur(1), prev(1), nxt(1), vec(CONV_WIDTH * SUBLANES), vec(1), vec(1)],
        out_specs=pl.BlockSpec((1, ts, CONV_CH), lambda bi, i: (bi, i, 0)),
        out_shape=jax.ShapeDtypeStruct((b, s, CONV_CH), BF16),
        scratch_shapes=[pltpu.VMEM((SUBLANES, ts + 2 * CONV_HALO, CONV_CH), F32)],
        compiler_params=_params(("parallel", "arbitrary")),
        name="conv_module",
    )(hcat, hcat, hcat, hcat, hcat, hcat, jnp.repeat(conv_w, SUBLANES, axis=0), conv_b, conv_norm)


def _qk_prep_kernel(q_ref, k_ref, cos_ref, sin_ref, qn_ref, kn_ref, qo_ref, ko_ref, *, scale):
    cos = cos_ref[...]
    sin = sin_ref[...]

    def prep(x_ref, g, o_ref, mul):
        for j in range(DA_QK // DA_HEAD_DIM):
            sl = slice(j * DA_HEAD_DIM, (j + 1) * DA_HEAD_DIM)
            xn = _rms(x_ref[0, :, sl].astype(F32), g)
            rot = pltpu.roll(xn, DA_HEAD_DIM // 2, axis=1)
            o_ref[0, :, sl] = ((xn * cos + rot * sin) * mul).astype(o_ref.dtype)

    prep(q_ref, qn_ref[...], qo_ref, scale)
    prep(k_ref, kn_ref[...], ko_ref, 1.0)


def qk_prep(hcat, cos, sin, q_norm, k_norm, ts=512):
    b, s, _ = hcat.shape
    assert s % ts == 0, (s, ts)
    col = lambda c: pl.BlockSpec((1, ts, DA_QK), lambda bi, i: (bi, i, c))
    tab = pl.BlockSpec((ts, DA_HEAD_DIM), lambda bi, i: (i, 0))
    vec = pl.BlockSpec((1, DA_HEAD_DIM), lambda bi, i: (0, 0))
    out = pl.BlockSpec((1, ts, DA_QK), lambda bi, i: (bi, i, 0))
    return pl.pallas_call(
        functools.partial(_qk_prep_kernel, scale=LOG2E / math.sqrt(DA_HEAD_DIM)),
        grid=(b, s // ts),
        in_specs=[col(2), col(3), tab, tab, vec, vec],
        out_specs=[out, out],
        out_shape=[jax.ShapeDtypeStruct((b, s, DA_QK), BF16)] * 2,
        compiler_params=_params(("parallel", "parallel")),
        name="qk_prep",
    )(hcat, hcat, cos, sin, q_norm, k_norm)


def _attn_kernel(q_ref, k_ref, v_ref, lam_ref, subln_ref, o_ref, m_ref, l_ref, acc_ref,
                 *, lambda_init, stabilize):
    ki = pl.program_id(3)
    d = DA_HEAD_DIM

    @pl.when(ki == 0)
    def _():
        if stabilize:
            m_ref[...] = jnp.full_like(m_ref, -jnp.inf)
        l_ref[...] = jnp.zeros_like(l_ref)
        acc_ref[...] = jnp.zeros_like(acc_ref)

    v = v_ref[0]
    for c in range(2):
        q = q_ref[0, :, c * d:(c + 1) * d]
        k = k_ref[0, :, c * d:(c + 1) * d]
        s = lax.dot_general(q, k, (((1,), (1,)), ((), ())), preferred_element_type=F32)
        if stabilize:
            m_prev = m_ref[c]
            m_new = jnp.maximum(m_prev, jnp.max(s, axis=-1, keepdims=True))
            alpha = jnp.exp2(m_prev - m_new)
            m_ref[c] = m_new
            p = jnp.exp2(s - m_new)
        else:
            p = jnp.exp2(s)
        psum = p[:, 0:LANES]
        for j in range(1, p.shape[1] // LANES):
            psum = psum + p[:, j * LANES:(j + 1) * LANES]
        pv = jnp.dot(p.astype(BF16), v, preferred_element_type=F32)
        if stabilize:
            l_ref[c] = alpha * l_ref[c] + psum
            acc_ref[c] = alpha * acc_ref[c] + pv
        else:
            l_ref[c] += psum
            acc_ref[c] += pv

    @pl.when(ki == pl.num_programs(3) - 1)
    def _():
        lp = lam_ref[...]
        lam = (jnp.exp(jnp.sum(lp[0:1] * lp[1:2], axis=-1, keepdims=True))
               - jnp.exp(jnp.sum(lp[2:3] * lp[3:4], axis=-1, keepdims=True)) + lambda_init)
        l0 = jnp.sum(l_ref[0], axis=-1, keepdims=True)
        l1 = jnp.sum(l_ref[1], axis=-1, keepdims=True)
        o = acc_ref[0] / l0 - lam * (acc_ref[1] / l1)
        o_ref[0] = (_rms(o, subln_ref[...]) * (1.0 - lambda_init)).astype(o_ref.dtype)


def logit_bound_log2(q_norm, k_norm):
    return (LOG2E * math.sqrt(DA_HEAD_DIM)) * jnp.max(jnp.abs(q_norm)) * jnp.max(jnp.abs(k_norm))


def diff_attention(qh, kh, hcat, lam_params, subln, lambda_init, stabilize, tq=1024, tk=4096):
    b, s, _ = qh.shape
    tq, tk = min(tq, s), min(tk, s)
    assert s % tq == 0 and s % tk == 0, (s, tq, tk)
    dv = 2 * DA_HEAD_DIM
    v_col0 = (hcat.shape[2] - DA_QK) // dv
    return pl.pallas_call(
        functools.partial(_attn_kernel, lambda_init=lambda_init, stabilize=stabilize),
        grid=(b, DA_HEADS, s // tq, s // tk),
        in_specs=[pl.BlockSpec((1, tq, dv), lambda bi, h, qi, ki: (bi, qi, h)),
                  pl.BlockSpec((1, tk, dv), lambda bi, h, qi, ki: (bi, ki, h)),
                  pl.BlockSpec((1, tk, dv), lambda bi, h, qi, ki: (bi, ki, v_col0 + h)),
                  pl.BlockSpec((4, DA_HEAD_DIM), lambda bi, h, qi, ki: (0, 0)),
                  pl.BlockSpec((1, dv), lambda bi, h, qi, ki: (0, 0))],
        out_specs=pl.BlockSpec((1, tq, dv), lambda bi, h, qi, ki: (bi, qi, h)),
        out_shape=jax.ShapeDtypeStruct((b, s, DA_QK), BF16),
        scratch_shapes=[pltpu.VMEM((2, tq, 1), F32), pltpu.VMEM((2, tq, LANES), F32),
                        pltpu.VMEM((2, tq, dv), F32)],
        compiler_params=_params(("parallel", "parallel", "parallel", "arbitrary")),
        name="diff_attention",
    )(qh, kh, hcat, lam_params, subln)


def _even_out_kernel(x_ref, a_ref, b_ref, w_ref, o_ref):
    ka = a_ref.shape[1]
    y = jnp.dot(a_ref[...], w_ref[0:ka, :], preferred_element_type=F32)
    y = y + jnp.dot(b_ref[...], w_ref[ka:, :], preferred_element_type=F32)
    o_ref[...] = x_ref[...] + y


def even_out(x, y_conv, y_att, w, tm=512):
    m, d = x.shape
    assert m % tm == 0, (m, tm)
    row = lambda width: pl.BlockSpec((tm, width), lambda i: (i, 0))
    return pl.pallas_call(
        _even_out_kernel,
        grid=(m // tm,),
        in_specs=[row(d), row(y_conv.shape[1]), row(y_att.shape[1]),
                  pl.BlockSpec(w.shape, lambda i: (0, 0))],
        out_specs=row(d),
        out_shape=jax.ShapeDtypeStruct((m, d), F32),
        compiler_params=_params(("parallel",)),
        name="even_out",
    )(x, y_conv, y_att, w)


def _odd_out_kernel(x_ref, hf_ref, hb_ref, og_ref, nrm_ref, w_ref, o_ref, z_ref):
    for h in range(ML_HEADS):
        sl = slice(h * ML_DV, (h + 1) * ML_DV)
        hs = hf_ref[:, sl].astype(F32) + hb_ref[:, sl].astype(F32)
        z = _rms(hs, nrm_ref[:, sl]) * jax.nn.sigmoid(og_ref[:, sl].astype(F32))
        z_ref[:, sl] = z.astype(BF16)
    o_ref[...] = x_ref[...] + jnp.dot(z_ref[...], w_ref[...], preferred_element_type=F32)


def odd_out(x, h_fwd, h_bwd, hcat, out_norm, w, tm=512):
    m, d = x.shape
    assert m % tm == 0, (m, tm)
    row = lambda c: pl.BlockSpec((tm, d), lambda i: (i, c))
    return pl.pallas_call(
        _odd_out_kernel,
        grid=(m // tm,),
        in_specs=[row(0), row(0), row(0), row(2),
                  pl.BlockSpec((1, d), lambda i: (0, 0)),
                  pl.BlockSpec(w.shape, lambda i: (0, 0))],
        out_specs=row(0),
        out_shape=jax.ShapeDtypeStruct((m, d), F32),
        scratch_shapes=[pltpu.VMEM((tm, d), BF16)],
        compiler_params=_params(("parallel",)),
        name="odd_out",
    )(x, h_fwd, h_bwd, hcat, out_norm, w)


def _ffn_kernel(x_ref, g_ref, wg_ref, wu_ref, wd_ref, o_ref, xn_ref):
    j = pl.program_id(1)

    @pl.when(j == 0)
    def _():
        x = x_ref[...]
        xn_ref[...] = _rms(x, g_ref[...]).astype(BF16)
        o_ref[...] = x

    xn = xn_ref[...]
    gate = jnp.dot(xn, wg_ref[...], preferred_element_type=F32)
    up = jnp.dot(xn, wu_ref[...], preferred_element_type=F32)
    hid = (gate * jax.nn.sigmoid(gate) * up).astype(BF16)
    o_ref[...] += jnp.dot(hid, wd_ref[...], preferred_element_type=F32)


def ffn(x, g, wg, wu, wd, tm=512, tf=512):
    m, d = x.shape
    f = wg.shape[1]
    assert m % tm == 0 and f % tf == 0, (m, f, tm, tf)
    return pl.pallas_call(
        _ffn_kernel,
        grid=(m // tm, f // tf),
        in_specs=[pl.BlockSpec((tm, d), lambda i, j: (i, 0)),
                  pl.BlockSpec((1, d), lambda i, j: (0, 0)),
                  pl.BlockSpec((d, tf), lambda i, j: (0, j)),
                  pl.BlockSpec((d, tf), lambda i, j: (0, j)),
                  pl.BlockSpec((tf, d), lambda i, j: (j, 0))],
        out_specs=pl.BlockSpec((tm, d), lambda i, j: (i, 0)),
        out_shape=jax.ShapeDtypeStruct((m, d), F32),
        scratch_shapes=[pltpu.VMEM((tm, d), BF16)],
        compiler_params=_params(("parallel", "arbitrary")),
        name="ffn",
    )(x, g, wg, wu, wd)


def _split3(x):
    x1 = x.astype(BF16)
    r = x - x1.astype(F32)
    x2 = r.astype(BF16)
    x3 = (r - x2.astype(F32)).astype(BF16)
    return x1, x2, x3


def _mlstm_kernel(qf_ref, ktf_ref, vf_ref, gf_ref, gtf_ref, qb_ref, ktb_ref, vb_ref, gb_ref, gtb_ref,
                  bc_ref, br_ref, hf_ref, hb_ref, c_ref, m_ref):
    @pl.when(pl.program_id(1) == 0)
    def _():
        c_ref[...] = jnp.zeros_like(c_ref)
        m_ref[...] = jnp.zeros_like(m_ref)

    _mlstm_direction(qf_ref, ktf_ref, vf_ref, gf_ref, gtf_ref, bc_ref, br_ref, hf_ref,
                     c_ref.at[0], m_ref.at[0], reverse=False)
    _mlstm_direction(qb_ref, ktb_ref, vb_ref, gb_ref, gtb_ref, bc_ref, br_ref, hb_ref,
                     c_ref.at[1], m_ref.at[1], reverse=True)


def _mlstm_direction(q_ref, kt_ref, v_ref, g_ref, gt_ref, bc_ref, br_ref, h_ref, c_ref, m_ref,
                     *, reverse):
    L = CHUNK
    t_idx = lax.broadcasted_iota(jnp.int32, (L, L), 0)
    s_idx = lax.broadcasted_iota(jnp.int32, (L, L), 1)
    mask = (s_idx >= t_idx) if reverse else (s_idx <= t_idx)
    mask_t = (s_idx <= t_idx) if reverse else (s_idx >= t_idx)
    tri_col = jnp.where(mask, 1.0, 0.0).astype(BF16)
    tri_row = jnp.where(mask_t, 1.0, 0.0).astype(BF16)
    last = 0 if reverse else L - 1

    g_col = g_ref[0][:, 0:ML_GATES] + br_ref[...]
    g_row = gt_ref[0] + bc_ref[...]
    lf_col = jax.nn.log_sigmoid(g_col)
    lf_row = jax.nn.log_sigmoid(g_row)
    bcum_col = sum(jnp.dot(tri_col, p, preferred_element_type=F32) for p in _split3(lf_col))
    bcum_row = sum(jnp.dot(p, tri_row, preferred_element_type=F32) for p in _split3(lf_row))

    it = 2 if reverse else 0
    for h in range(ML_HEADS):
        ci = it * ML_HEADS + h
        cf = (it + 1) * ML_HEADS + h
        i_row = g_row[ci:ci + 1, :]
        b_col = bcum_col[:, cf:cf + 1]
        b_row = bcum_row[cf:cf + 1, :]
        total = b_col[last:last + 1, :]
        m_prev = m_ref[h]

        q = q_ref[0, :, h * ML_DK:(h + 1) * ML_DK]
        kt = kt_ref[0, h * ML_DK:(h + 1) * ML_DK, :]
        v_aug = jnp.concatenate([v_ref[0, :, h * ML_DV:(h + 1) * ML_DV], jnp.ones((L, LANES), BF16)], axis=1)

        dlog = jnp.where(mask, b_col - b_row + i_row, -jnp.inf)
        inter = b_col + m_prev
        mt = jnp.maximum(inter, jnp.max(dlog, axis=-1, keepdims=True))
        dmat = jnp.exp(dlog - mt)
        qk = jnp.dot(q, kt, preferred_element_type=F32)
        sc = qk * dmat * (1.0 / math.sqrt(ML_DK))
        w_int = jnp.exp(inter - mt)
        qc = jnp.dot(q, c_ref[h].astype(BF16), preferred_element_type=F32)
        full = jnp.dot(sc.astype(BF16), v_aug, preferred_element_type=F32) + w_int * qc
        den = full[:, ML_DV:]
        inv = 1.0 / jnp.maximum(jnp.abs(den), jnp.exp(-mt))
        hout = full[:, :ML_DV] * jnp.concatenate([inv] * (ML_DV // LANES), axis=1)
        h_ref[0, :, h * ML_DV:(h + 1) * ML_DV] = hout.astype(h_ref.dtype)

        m_new = mt[last:last + 1, :]
        w_s = jnp.exp(total - b_row + i_row - m_new) * (1.0 / math.sqrt(ML_DK))
        decay = jnp.exp(total + m_prev - m_new)
        kwt = (kt.astype(F32) * w_s).astype(BF16)
        c_ref[h] = decay * c_ref[h] + jnp.dot(kwt, v_aug, preferred_element_type=F32)
        m_ref[h] = m_new


def mlstm_scan(hcat, k_t, gates, gates_t, bias_col, bias_row):
    b, s, _ = hcat.shape
    assert s % CHUNK == 0, (s, CHUNK)
    nc = s // CHUNK
    nq = ML_HEADS * ML_DK
    nv = ML_HEADS * ML_DV
    fwd = lambda c: c
    bwd = lambda c: nc - 1 - c

    def direction_specs(pos):
        return [pl.BlockSpec((1, CHUNK, nq), lambda bi, c: (bi, pos(c), 0)),
                pl.BlockSpec((1, nq, CHUNK), lambda bi, c: (bi, 0, pos(c))),
                pl.BlockSpec((1, CHUNK, nv), lambda bi, c: (bi, pos(c), 1)),
                pl.BlockSpec((1, CHUNK, LANES), lambda bi, c: (bi, pos(c), 0)),
                pl.BlockSpec((1, ML_GATES, CHUNK), lambda bi, c: (bi, 0, pos(c)))]

    out_spec = lambda pos: pl.BlockSpec((1, CHUNK, nv), lambda bi, c: (bi, pos(c), 0))
    operands = (hcat, k_t, hcat, gates, gates_t)
    return pl.pallas_call(
        _mlstm_kernel,
        grid=(b, nc),
        in_specs=direction_specs(fwd) + direction_specs(bwd)
        + [pl.BlockSpec((ML_GATES, 1), lambda bi, c: (0, 0)),
           pl.BlockSpec((1, ML_GATES), lambda bi, c: (0, 0))],
        out_specs=[out_spec(fwd), out_spec(bwd)],
        out_shape=[jax.ShapeDtypeStruct((b, s, nv), BF16)] * 2,
        scratch_shapes=[pltpu.VMEM((2, ML_HEADS, ML_DK, ML_DV + LANES), F32),
                        pltpu.VMEM((2, ML_HEADS, 1, 1), F32)],
        compiler_params=_params(("parallel", "arbitrary")),
        name="mlstm",
    )(*operands, *operands, bias_col, bias_row)


def _rope_tables(s):
    d = DA_HEAD_DIM
    inv = 1.0 / (ROPE_THETA ** (jnp.arange(0, d, 2, dtype=F32) / d))
    ang = jnp.arange(s, dtype=F32)[:, None] * inv[None, :]
    cos, sin = jnp.cos(ang), jnp.sin(ang)
    return jnp.concatenate([cos, cos], axis=-1), jnp.concatenate([-sin, sin], axis=-1)


def _even_layer(x, b, s, p):
    m = b * s
    hcat = norm_matmul(x, p["norm_mix"], p["w_in"]).reshape(b, s, -1)
    y_conv = conv_module(hcat, p["conv_w"], p["conv_b"], p["conv_norm"])
    cos, sin = _rope_tables(s)
    qh, kh = qk_prep(hcat, cos, sin, p["q_norm"], p["k_norm"])
    attn = lambda stabilize: functools.partial(diff_attention, lam_params=p["lam"], subln=p["subln"],
                                               lambda_init=p["lambda_init"], stabilize=stabilize)
    y_att = lax.cond(logit_bound_log2(p["q_norm"], p["k_norm"]) < SAFE_LOGIT_LOG2,
                     attn(False), attn(True), qh, kh, hcat)
    return even_out(x, y_conv.reshape(m, -1), y_att.reshape(m, -1), p["w_out"])


def _odd_layer(x, b, s, p):
    m = b * s
    hcat, gates = norm_matmul(x, p["norm_mix"], p["w_main"], w_side=p["w_gates"])
    gates = gates.reshape(b, s, LANES)
    gates_t = jnp.swapaxes(gates[:, :, :ML_GATES], 1, 2)
    hcat3 = hcat.reshape(b, s, -1)
    nq = ML_HEADS * ML_DK
    k_t = jnp.swapaxes(hcat3[:, :, nq:2 * nq], 1, 2)
    h_fwd, h_bwd = mlstm_scan(hcat3, k_t, gates, gates_t, p["bias_col"], p["bias_row"])
    return odd_out(x, h_fwd.reshape(m, -1), h_bwd.reshape(m, -1), hcat, p["out_norm"], p["w_out"])


def kernel(x_prompt, x_sample, norm_mix, norm_ffn, ev_w_in, ev_conv_w, ev_conv_b, ev_conv_norm, ev_q_norm, ev_k_norm, ev_lambda, ev_subln, ev_w_out, od_w_in, od_gate_b, od_out_norm, od_w_out, ffn_w_gate, ffn_w_up, ffn_w_down):
    depth = norm_mix.shape[0]
    layers = []
    for i in range(depth):
        j = i // 2
        p = {"norm_mix": norm_mix[i][None], "norm_ffn": norm_ffn[i][None],
             "wg": ffn_w_gate[i].astype(BF16), "wu": ffn_w_up[i].astype(BF16),
             "wd": ffn_w_down[i].astype(BF16)}
        if i % 2 == 0:
            p.update({"w_in": ev_w_in[j].astype(BF16), "conv_w": ev_conv_w[j], "conv_b": ev_conv_b[j][None],
                      "conv_norm": ev_conv_norm[j][None], "q_norm": ev_q_norm[j][None],
                      "k_norm": ev_k_norm[j][None], "lam": ev_lambda[j], "subln": ev_subln[j][None],
                      "w_out": ev_w_out[j].astype(BF16),
                      "lambda_init": 0.8 - 0.6 * math.exp(-0.3 * i)})
        else:
            w_gates = jnp.pad(od_w_in[j][:, ML_MAIN:], ((0, 0), (0, LANES - ML_GATES)))
            p.update({"w_main": od_w_in[j][:, :ML_MAIN].astype(BF16), "w_gates": w_gates.astype(BF16),
                      "bias_col": od_gate_b[j].reshape(ML_GATES, 1),
                      "bias_row": od_gate_b[j].reshape(1, ML_GATES),
                      "out_norm": od_out_norm[j].reshape(1, D_MODEL),
                      "w_out": od_w_out[j].astype(BF16)})
        layers.append(p)

    outs = []
    for x3 in (x_prompt, x_sample):
        b, s, d = x3.shape
        x = x3.reshape(b * s, d)
        for i, p in enumerate(layers):
            x = _even_layer(x, b, s, p) if i % 2 == 0 else _odd_layer(x, b, s, p)
            x = ffn(x, p["norm_ffn"], p["wg"], p["wu"], p["wd"])
        outs.append(x.reshape(b, s, d))
    return tuple(outs)
```

```python
import functools
import math

import jax
import jax.numpy as jnp
from jax import lax
from jax.experimental import pallas as pl
from jax.experimental.pallas import tpu as pltpu

F32 = jnp.float32
BF16 = jnp.bfloat16

D_MODEL = 2048
CONV_CH = D_MODEL // 2
CONV_WIDTH = 31
CONV_PAD = CONV_WIDTH // 2
CONV_HALO = 16
DA_HEAD_DIM = 128
DA_HEADS = D_MODEL // (4 * DA_HEAD_DIM)
DA_QK = DA_HEADS * 2 * DA_HEAD_DIM
ML_HEADS = 4
ML_DV = D_MODEL // ML_HEADS
ML_DK = ML_DV // 2
ML_GATES = 4 * ML_HEADS
ML_MAIN = 2 * ML_HEADS * ML_DK + 2 * ML_HEADS * ML_DV
FFN_HIDDEN = ((8 * D_MODEL // 3 + 255) // 256) * 256
CHUNK = 128
ROPE_THETA = 10000.0
EPS = 1e-6
LANES = 128
SUBLANES = 8
LOG2E = math.log2(math.e)
SAFE_LOGIT_LOG2 = 64.0
VMEM_LIMIT = 56 * 1024 * 1024


def _params(semantics, vmem_limit=VMEM_LIMIT):
    return pltpu.CompilerParams(dimension_semantics=semantics, vmem_limit_bytes=vmem_limit)


def _rms(x, g):
    ms = jnp.mean(x * x, axis=-1, keepdims=True)
    return x * lax.rsqrt(ms + EPS) * g


def _norm_matmul_kernel(x_ref, g_ref, w_ref, o_ref, xn_ref, w_side_ref=None, o_side_ref=None):
    @pl.when(pl.program_id(1) == 0)
    def _():
        xn = _rms(x_ref[...], g_ref[...]).astype(BF16)
        xn_ref[...] = xn
        if w_side_ref is not None:
            o_side_ref[...] = jnp.dot(xn, w_side_ref[...], preferred_element_type=F32)

    o_ref[...] = jnp.dot(xn_ref[...], w_ref[...], preferred_element_type=F32).astype(o_ref.dtype)


def _norm_matmul_side_kernel(x_ref, g_ref, w_ref, w_side_ref, o_ref, o_side_ref, xn_ref):
    _norm_matmul_kernel(x_ref, g_ref, w_ref, o_ref, xn_ref, w_side_ref, o_side_ref)


def norm_matmul(x, g, w, w_side=None, tm=1024, tn=1024):
    m, d = x.shape
    n = w.shape[1]
    tm, tn = min(tm, m), min(tn, n)
    assert m % tm == 0 and n % tn == 0, (m, n, tm, tn)
    in_specs = [pl.BlockSpec((tm, d), lambda i, j: (i, 0)),
                pl.BlockSpec((1, d), lambda i, j: (0, 0)),
                pl.BlockSpec((d, tn), lambda i, j: (0, j))]
    out_specs = [pl.BlockSpec((tm, tn), lambda i, j: (i, j))]
    out_shape = [jax.ShapeDtypeStruct((m, n), BF16)]
    operands = [x, g, w]
    if w_side is not None:
        ns = w_side.shape[1]
        in_specs.append(pl.BlockSpec((d, ns), lambda i, j: (0, 0)))
        out_specs.append(pl.BlockSpec((tm, ns), lambda i, j: (i, 0)))
        out_shape.append(jax.ShapeDtypeStruct((m, ns), F32))
        operands.append(w_side)
    outs = pl.pallas_call(
        _norm_matmul_kernel if w_side is None else _norm_matmul_side_kernel,
        grid=(m // tm, n // tn),
        in_specs=in_specs,
        out_specs=out_specs,
        out_shape=out_shape,
        scratch_shapes=[pltpu.VMEM((tm, d), BF16)],
        compiler_params=_params(("parallel", "arbitrary")),
        name="norm_matmul",
    )(*operands)
    return outs[0] if w_side is None else outs


def _conv_kernel(ac_ref, ap_ref, an_ref, gc_ref, gp_ref, gn_ref, w_ref, b_ref, nrm_ref, o_ref, u_ref,
                 *, ts, tr):
    i = pl.program_id(1)
    n = pl.num_programs(1)

    def glu(a_ref, g_ref):
        return a_ref[0].astype(F32) * jax.nn.sigmoid(g_ref[0].astype(F32))

    u_ref[0, 0:CONV_HALO, :] = jnp.where(i > 0, glu(ap_ref, gp_ref), 0.0)
    u_ref[0, CONV_HALO:CONV_HALO + ts, :] = glu(ac_ref, gc_ref)
    u_ref[0, CONV_HALO + ts:, :] = jnp.where(i < n - 1, glu(an_ref, gn_ref), 0.0)
    rows = ts + 2 * CONV_HALO - SUBLANES
    for sh in range(1, SUBLANES):
        u_ref[sh, 0:rows, :] = u_ref[0, sh:sh + rows, :]

    bias = b_ref[...]
    nrm = nrm_ref[...]
    base = CONV_HALO - CONV_PAD
    for r in range(ts // tr):
        acc = jnp.zeros((tr, CONV_CH), F32)
        for k in range(CONV_WIDTH):
            sh, lo = (base + k) % SUBLANES, r * tr + (base + k) // SUBLANES * SUBLANES
            w_tap = w_ref[k * SUBLANES:(k + 1) * SUBLANES, :]
            acc = acc + u_ref[sh, lo:lo + tr, :] * jnp.concatenate([w_tap] * (tr // SUBLANES), axis=0)
        y = _rms(acc + bias, nrm)
        o_ref[0, r * tr:(r + 1) * tr, :] = (y * jax.nn.sigmoid(y)).astype(o_ref.dtype)


def conv_module(hcat, conv_w, conv_b, conv_norm, ts=256, tr=32):
    b, s, _ = hcat.shape
    assert s % ts == 0 and ts % tr == 0 and ts % CONV_HALO == 0, (s, ts, tr)
    nh = ts // CONV_HALO
    last = s // CONV_HALO - 1
    cur = lambda col: pl.BlockSpec((1, ts, CONV_CH), lambda bi, i: (bi, i, col))
    prev = lambda col: pl.BlockSpec((1, CONV_HALO, CONV_CH),
                                    lambda bi, i: (bi, jnp.maximum(i * nh - 1, 0), col))
    nxt = lambda col: pl.BlockSpec((1, CONV_HALO, CONV_CH),
                                   lambda bi, i: (bi, jnp.minimum((i + 1) * nh, last), col))
    vec = lambda rows: pl.BlockSpec((rows, CONV_CH), lambda bi, i: (0, 0))
    return pl.pallas_call(
        functools.partial(_conv_kernel, ts=ts, tr=tr),
        grid=(b, s // ts),
        in_specs=[cur(0), prev(0), nxt(0), cur(1), prev(1), nxt(1), vec(CONV_WIDTH * SUBLANES), vec(1), vec(1)],
        out_specs=pl.BlockSpec((1, ts, CONV_CH), lambda bi, i: (bi, i, 0)),
        out_shape=jax.ShapeDtypeStruct((b, s, CONV_CH), BF16),
        scratch_shapes=[pltpu.VMEM((SUBLANES, ts + 2 * CONV_HALO, CONV_CH), F32)],
        compiler_params=_params(("parallel", "arbitrary")),
        name="conv_module",
    )(hcat, hcat, hcat, hcat, hcat, hcat, jnp.repeat(conv_w, SUBLANES, axis=0), conv_b, conv_norm)


def _qk_prep_kernel(q_ref, k_ref, cos_ref, sin_ref, qn_ref, kn_ref, qo_ref, ko_ref, *, scale):
    cos = cos_ref[...]
    sin = sin_ref[...]

    def prep(x_ref, g, o_ref, mul):
        for j in range(DA_QK // DA_HEAD_DIM):
            sl = slice(j * DA_HEAD_DIM, (j + 1) * DA_HEAD_DIM)
            xn = _rms(x_ref[0, :, sl].astype(F32), g)
            rot = pltpu.roll(xn, DA_HEAD_DIM // 2, axis=1)
            o_ref[0, :, sl] = ((xn * cos + rot * sin) * mul).astype(o_ref.dtype)

    prep(q_ref, qn_ref[...], qo_ref, scale)
    prep(k_ref, kn_ref[...], ko_ref, 1.0)


def qk_prep(hcat, cos, sin, q_norm, k_norm, ts=512):
    b, s, _ = hcat.shape
    assert s % ts == 0, (s, ts)
    col = lambda c: pl.BlockSpec((1, ts, DA_QK), lambda bi, i: (bi, i, c))
    tab = pl.BlockSpec((ts, DA_HEAD_DIM), lambda bi, i: (i, 0))
    vec = pl.BlockSpec((1, DA_HEAD_DIM), lambda bi, i: (0, 0))
    out = pl.BlockSpec((1, ts, DA_QK), lambda bi, i: (bi, i, 0))
    return pl.pallas_call(
        functools.partial(_qk_prep_kernel, scale=LOG2E / math.sqrt(DA_HEAD_DIM)),
        grid=(b, s // ts),
        in_specs=[col(2), col(3), tab, tab, vec, vec],
        out_specs=[out, out],
        out_shape=[jax.ShapeDtypeStruct((b, s, DA_QK), BF16)] * 2,
        compiler_params=_params(("parallel", "parallel")),
        name="qk_prep",
    )(hcat, hcat, cos, sin, q_norm, k_norm)


def _attn_kernel(q_ref, k_ref, v_ref, lam_ref, subln_ref, o_ref, m_ref, l_ref, acc_ref,
                 *, lambda_init, stabilize):
    ki = pl.program_id(3)
    d = DA_HEAD_DIM

    @pl.when(ki == 0)
    def _():
        if stabilize:
            m_ref[...] = jnp.full_like(m_ref, -jnp.inf)
        l_ref[...] = jnp.zeros_like(l_ref)
        acc_ref[...] = jnp.zeros_like(acc_ref)

    v = v_ref[0]
    for c in range(2):
        q = q_ref[0, :, c * d:(c + 1) * d]
        k = k_ref[0, :, c * d:(c + 1) * d]
        s = lax.dot_general(q, k, (((1,), (1,)), ((), ())), preferred_element_type=F32)
        if stabilize:
            m_prev = m_ref[c]
            m_new = jnp.maximum(m_prev, jnp.max(s, axis=-1, keepdims=True))
            alpha = jnp.exp2(m_prev - m_new)
            m_ref[c] = m_new
            p = jnp.exp2(s - m_new)
        else:
            p = jnp.exp2(s)
        psum = p[:, 0:LANES]
        for j in range(1, p.shape[1] // LANES):
            psum = psum + p[:, j * LANES:(j + 1) * LANES]
        pv = jnp.dot(p.astype(BF16), v, preferred_element_type=F32)
        if stabilize:
            l_ref[c] = alpha * l_ref[c] + psum
            acc_ref[c] = alpha * acc_ref[c] + pv
        else:
            l_ref[c] += psum
            acc_ref[c] += pv

    @pl.when(ki == pl.num_programs(3) - 1)
    def _():
        lp = lam_ref[...]
        lam = (jnp.exp(jnp.sum(lp[0:1] * lp[1:2], axis=-1, keepdims=True))
               - jnp.exp(jnp.sum(lp[2:3] * lp[3:4], axis=-1, keepdims=True)) + lambda_init)
        l0 = jnp.sum(l_ref[0], axis=-1, keepdims=True)
        l1 = jnp.sum(l_ref[1], axis=-1, keepdims=True)
        o = acc_ref[0] / l0 - lam * (acc_ref[1] / l1)
        o_ref[0] = (_rms(o, subln_ref[...]) * (1.0 - lambda_init)).astype(o_ref.dtype)


def logit_bound_log2(q_norm, k_norm):
    return (LOG2E * math.sqrt(DA_HEAD_DIM)) * jnp.max(jnp.abs(q_norm)) * jnp.max(jnp.abs(k_norm))


def diff_attention(qh, kh, hcat, lam_params, subln, lambda_init, stabilize, tq=1024, tk=4096):
    b, s, _ = qh.shape
    tq, tk = min(tq, s), min(tk, s)
    assert s % tq == 0 and s % tk == 0, (s, tq, tk)
    dv = 2 * DA_HEAD_DIM
    v_col0 = (hcat.shape[2] - DA_QK) // dv
    return pl.pallas_call(
        functools.partial(_attn_kernel, lambda_init=lambda_init, stabilize=stabilize),
        grid=(b, DA_HEADS, s // tq, s // tk),
        in_specs=[pl.BlockSpec((1, tq, dv), lambda bi, h, qi, ki: (bi, qi, h)),
                  pl.BlockSpec((1, tk, dv), lambda bi, h, qi, ki: (bi, ki, h)),
                  pl.BlockSpec((1, tk, dv), lambda bi, h, qi, ki: (bi, ki, v_col0 + h)),
                  pl.BlockSpec((4, DA_HEAD_DIM), lambda bi, h, qi, ki: (0, 0)),
                  pl.BlockSpec((1, dv), lambda bi, h, qi, ki: (0, 0))],
        out_specs=pl.BlockSpec((1, tq, dv), lambda bi, h, qi, ki: (bi, qi, h)),
        out_shape=jax.ShapeDtypeStruct((b, s, DA_QK), BF16),
        scratch_shapes=[pltpu.VMEM((2, tq, 1), F32), pltpu.VMEM((2, tq, LANES), F32),
                        pltpu.VMEM((2, tq, dv), F32)],
        compiler_params=_params(("parallel", "parallel", "parallel", "arbitrary")),
        name="diff_attention",
    )(qh, kh, hcat, lam_params, subln)


def _even_out_kernel(x_ref, a_ref, b_ref, w_ref, o_ref):
    ka = a_ref.shape[1]
    y = jnp.dot(a_ref[...], w_ref[0:ka, :], preferred_element_type=F32)
    y = y + jnp.dot(b_ref[...], w_ref[ka:, :], preferred_element_type=F32)
    o_ref[...] = x_ref[...] + y


def even_out(x, y_conv, y_att, w, tm=512):
    m, d = x.shape
    assert m % tm == 0, (m, tm)
    row = lambda width: pl.BlockSpec((tm, width), lambda i: (i, 0))
    return pl.pallas_call(
        _even_out_kernel,
        grid=(m // tm,),
        in_specs=[row(d), row(y_conv.shape[1]), row(y_att.shape[1]),
                  pl.BlockSpec(w.shape, lambda i: (0, 0))],
        out_specs=row(d),
        out_shape=jax.ShapeDtypeStruct((m, d), F32),
        compiler_params=_params(("parallel",)),
        name="even_out",
    )(x, y_conv, y_att, w)


def _odd_out_kernel(x_ref, hf_ref, hb_ref, og_ref, nrm_ref, w_ref, o_ref, z_ref):
    for h in range(ML_HEADS):
        sl = slice(h * ML_DV, (h + 1) * ML_DV)
        hs = hf_ref[:, sl].astype(F32) + hb_ref[:, sl].astype(F32)
        z = _rms(hs, nrm_ref[:, sl]) * jax.nn.sigmoid(og_ref[:, sl].astype(F32))
        z_ref[:, sl] = z.astype(BF16)
    o_ref[...] = x_ref[...] + jnp.dot(z_ref[...], w_ref[...], preferred_element_type=F32)


def odd_out(x, h_fwd, h_bwd, hcat, out_norm, w, tm=512):
    m, d = x.shape
    assert m % tm == 0, (m, tm)
    row = lambda c: pl.BlockSpec((tm, d), lambda i: (i, c))
    return pl.pallas_call(
        _odd_out_kernel,
        grid=(m // tm,),
        in_specs=[row(0), row(0), row(0), row(2),
                  pl.BlockSpec((1, d), lambda i: (0, 0)),
                  pl.BlockSpec(w.shape, lambda i: (0, 0))],
        out_specs=row(0),
        out_shape=jax.ShapeDtypeStruct((m, d), F32),
        scratch_shapes=[pltpu.VMEM((tm, d), BF16)],
        compiler_params=_params(("parallel",)),
        name="odd_out",
    )(x, h_fwd, h_bwd, hcat, out_norm, w)


def _ffn_kernel(x_ref, g_ref, wg_ref, wu_ref, wd_ref, o_ref, xn_ref):
    j = pl.program_id(1)

    @pl.when(j == 0)
    def _():
        x = x_ref[...]
        xn_ref[...] = _rms(x, g_ref[...]).astype(BF16)
        o_ref[...] = x

    xn = xn_ref[...]
    gate = jnp.dot(xn, wg_ref[...], preferred_element_type=F32)
    up = jnp.dot(xn, wu_ref[...], preferred_element_type=F32)
    hid = (gate * jax.nn.sigmoid(gate) * up).astype(BF16)
    o_ref[...] += jnp.dot(hid, wd_ref[...], preferred_element_type=F32)


def ffn(x, g, wg, wu, wd, tm=1024, tf=512):
    m, d = x.shape
    f = wg.shape[1]
    tm = min(tm, m)
    assert m % tm == 0 and f % tf == 0, (m, f, tm, tf)
    vmem = 2 * 2 * tm * d * 4 + tm * d * 2 + 2 * 3 * d * tf * 2 + 3 * tm * tf * 4
    return pl.pallas_call(
        _ffn_kernel,
        grid=(m // tm, f // tf),
        in_specs=[pl.BlockSpec((tm, d), lambda i, j: (i, 0)),
                  pl.BlockSpec((1, d), lambda i, j: (0, 0)),
                  pl.BlockSpec((d, tf), lambda i, j: (0, j)),
                  pl.BlockSpec((d, tf), lambda i, j: (0, j)),
                  pl.BlockSpec((tf, d), lambda i, j: (j, 0))],
        out_specs=pl.BlockSpec((tm, d), lambda i, j: (i, 0)),
        out_shape=jax.ShapeDtypeStruct((m, d), F32),
        scratch_shapes=[pltpu.VMEM((tm, d), BF16)],
        compiler_params=_params(("parallel", "arbitrary"), vmem_limit=max(VMEM_LIMIT, vmem)),
        name="ffn",
    )(x, g, wg, wu, wd)


def _split3(x):
    x1 = x.astype(BF16)
    r = x - x1.astype(F32)
    x2 = r.astype(BF16)
    x3 = (r - x2.astype(F32)).astype(BF16)
    return x1, x2, x3


def _mlstm_kernel(qf_ref, ktf_ref, vf_ref, gf_ref, gtf_ref, qb_ref, ktb_ref, vb_ref, gb_ref, gtb_ref,
                  bc_ref, br_ref, hf_ref, hb_ref, c_ref, m_ref):
    @pl.when(pl.program_id(1) == 0)
    def _():
        c_ref[...] = jnp.zeros_like(c_ref)
        m_ref[...] = jnp.zeros_like(m_ref)

    _mlstm_direction(qf_ref, ktf_ref, vf_ref, gf_ref, gtf_ref, bc_ref, br_ref, hf_ref,
                     c_ref.at[0], m_ref.at[0], reverse=False)
    _mlstm_direction(qb_ref, ktb_ref, vb_ref, gb_ref, gtb_ref, bc_ref, br_ref, hb_ref,
                     c_ref.at[1], m_ref.at[1], reverse=True)


def _mlstm_direction(q_ref, kt_ref, v_ref, g_ref, gt_ref, bc_ref, br_ref, h_ref, c_ref, m_ref,
                     *, reverse):
    L = CHUNK
    t_idx = lax.broadcasted_iota(jnp.int32, (L, L), 0)
    s_idx = lax.broadcasted_iota(jnp.int32, (L, L), 1)
    mask = (s_idx >= t_idx) if reverse else (s_idx <= t_idx)
    mask_t = (s_idx <= t_idx) if reverse else (s_idx >= t_idx)
    tri_col = jnp.where(mask, 1.0, 0.0).astype(BF16)
    tri_row = jnp.where(mask_t, 1.0, 0.0).astype(BF16)
    last = 0 if reverse else L - 1

    g_col = g_ref[0][:, 0:ML_GATES] + br_ref[...]
    g_row = gt_ref[0] + bc_ref[...]
    lf_col = jax.nn.log_sigmoid(g_col)
    lf_row = jax.nn.log_sigmoid(g_row)
    bcum_col = sum(jnp.dot(tri_col, p, preferred_element_type=F32) for p in _split3(lf_col))
    bcum_row = sum(jnp.dot(p, tri_row, preferred_element_type=F32) for p in _split3(lf_row))

    it = 2 if reverse else 0
    for h in range(ML_HEADS):
        ci = it * ML_HEADS + h
        cf = (it + 1) * ML_HEADS + h
        i_row = g_row[ci:ci + 1, :]
        b_col = bcum_col[:, cf:cf + 1]
        b_row = bcum_row[cf:cf + 1, :]
        total = b_col[last:last + 1, :]
        m_prev = m_ref[h]

        q = q_ref[0, :, h * ML_DK:(h + 1) * ML_DK]
        kt = kt_ref[0, h * ML_DK:(h + 1) * ML_DK, :]
        v_aug = jnp.concatenate([v_ref[0, :, h * ML_DV:(h + 1) * ML_DV], jnp.ones((L, LANES), BF16)], axis=1)

        dlog = jnp.where(mask, b_col - b_row + i_row, -jnp.inf)
        inter = b_col + m_prev
        mt = jnp.maximum(inter, jnp.max(dlog, axis=-1, keepdims=True))
        dmat = jnp.exp(dlog - mt)
        qk = jnp.dot(q, kt, preferred_element_type=F32)
        sc = qk * dmat * (1.0 / math.sqrt(ML_DK))
        w_int = jnp.exp(inter - mt)
        qc = jnp.dot(q, c_ref[h].astype(BF16), preferred_element_type=F32)
        full = jnp.dot(sc.astype(BF16), v_aug, preferred_element_type=F32) + w_int * qc
        den = full[:, ML_DV:]
        inv = 1.0 / jnp.maximum(jnp.abs(den), jnp.exp(-mt))
        hout = full[:, :ML_DV] * jnp.concatenate([inv] * (ML_DV // LANES), axis=1)
        h_ref[0, :, h * ML_DV:(h + 1) * ML_DV] = hout.astype(h_ref.dtype)

        m_new = mt[last:last + 1, :]
        w_s = jnp.exp(total - b_row + i_row - m_new) * (1.0 / math.sqrt(ML_DK))
        decay = jnp.exp(total + m_prev - m_new)
        kwt = (kt.astype(F32) * w_s).astype(BF16)
        c_ref[h] = decay * c_ref[h] + jnp.dot(kwt, v_aug, preferred_element_type=F32)
        m_ref[h] = m_new


def mlstm_scan(hcat, k_t, gates, gates_t, bias_col, bias_row):
    b, s, _ = hcat.shape
    assert s % CHUNK == 0, (s, CHUNK)
    nc = s // CHUNK
    nq = ML_HEADS * ML_DK
    nv = ML_HEADS * ML_DV
    fwd = lambda c: c
    bwd = lambda c: nc - 1 - c

    def direction_specs(pos):
        return [pl.BlockSpec((1, CHUNK, nq), lambda bi, c: (bi, pos(c), 0)),
                pl.BlockSpec((1, nq, CHUNK), lambda bi, c: (bi, 0, pos(c))),
                pl.BlockSpec((1, CHUNK, nv), lambda bi, c: (bi, pos(c), 1)),
                pl.BlockSpec((1, CHUNK, LANES), lambda bi, c: (bi, pos(c), 0)),
                pl.BlockSpec((1, ML_GATES, CHUNK), lambda bi, c: (bi, 0, pos(c)))]

    out_spec = lambda pos: pl.BlockSpec((1, CHUNK, nv), lambda bi, c: (bi, pos(c), 0))
    operands = (hcat, k_t, hcat, gates, gates_t)
    return pl.pallas_call(
        _mlstm_kernel,
        grid=(b, nc),
        in_specs=direction_specs(fwd) + direction_specs(bwd)
        + [pl.BlockSpec((ML_GATES, 1), lambda bi, c: (0, 0)),
           pl.BlockSpec((1, ML_GATES), lambda bi, c: (0, 0))],
        out_specs=[out_spec(fwd), out_spec(bwd)],
        out_shape=[jax.ShapeDtypeStruct((b, s, nv), BF16)] * 2,
        scratch_shapes=[pltpu.VMEM((2, ML_HEADS, ML_DK, ML_DV + LANES), F32),
                        pltpu.VMEM((2, ML_HEADS, 1, 1), F32)],
        compiler_params=_params(("parallel", "arbitrary")),
        name="mlstm",
    )(*operands, *operands, bias_col, bias_row)


def _rope_tables(s):
    d = DA_HEAD_DIM
    inv = 1.0 / (ROPE_THETA ** (jnp.arange(0, d, 2, dtype=F32) / d))
    ang = jnp.arange(s, dtype=F32)[:, None] * inv[None, :]
    cos, sin = jnp.cos(ang), jnp.sin(ang)
    return jnp.concatenate([cos, cos], axis=-1), jnp.concatenate([-sin, sin], axis=-1)


def _even_layer(x, b, s, p):
    m = b * s
    hcat = norm_matmul(x, p["norm_mix"], p["w_in"], tn=p["w_in"].shape[1] // 2).reshape(b, s, -1)
    y_conv = conv_module(hcat, p["conv_w"], p["conv_b"], p["conv_norm"])
    cos, sin = _rope_tables(s)
    qh, kh = qk_prep(hcat, cos, sin, p["q_norm"], p["k_norm"])
    attn = lambda stabilize: functools.partial(diff_attention, lam_params=p["lam"], subln=p["subln"],
                                               lambda_init=p["lambda_init"], stabilize=stabilize)
    y_att = lax.cond(logit_bound_log2(p["q_norm"], p["k_norm"]) < SAFE_LOGIT_LOG2,
                     attn(False), attn(True), qh, kh, hcat)
    return even_out(x, y_conv.reshape(m, -1), y_att.reshape(m, -1), p["w_out"])


def _odd_layer(x, b, s, p):
    m = b * s
    hcat, gates = norm_matmul(x, p["norm_mix"], p["w_main"], w_side=p["w_gates"], tn=p["w_main"].shape[1] // 3)
    gates = gates.reshape(b, s, LANES)
    gates_t = jnp.swapaxes(gates[:, :, :ML_GATES], 1, 2)
    hcat3 = hcat.reshape(b, s, -1)
    nq = ML_HEADS * ML_DK
    k_t = jnp.swapaxes(hcat3[:, :, nq:2 * nq], 1, 2)
    h_fwd, h_bwd = mlstm_scan(hcat3, k_t, gates, gates_t, p["bias_col"], p["bias_row"])
    return odd_out(x, h_fwd.reshape(m, -1), h_bwd.reshape(m, -1), hcat, p["out_norm"], p["w_out"])


def kernel(x_prompt, x_sample, norm_mix, norm_ffn, ev_w_in, ev_conv_w, ev_conv_b, ev_conv_norm, ev_q_norm, ev_k_norm, ev_lambda, ev_subln, ev_w_out, od_w_in, od_gate_b, od_out_norm, od_w_out, ffn_w_gate, ffn_w_up, ffn_w_down):
    depth = norm_mix.shape[0]
    layers = []
    for i in range(depth):
        j = i // 2
        p = {"norm_mix": norm_mix[i][None], "norm_ffn": norm_ffn[i][None],
             "wg": ffn_w_gate[i].astype(BF16), "wu": ffn_w_up[i].astype(BF16),
             "wd": ffn_w_down[i].astype(BF16)}
        if i % 2 == 0:
            p.update({"w_in": ev_w_in[j].astype(BF16), "conv_w": ev_conv_w[j], "conv_b": ev_conv_b[j][None],
                      "conv_norm": ev_conv_norm[j][None], "q_norm": ev_q_norm[j][None],
                      "k_norm": ev_k_norm[j][None], "lam": ev_lambda[j], "subln": ev_subln[j][None],
                      "w_out": ev_w_out[j].astype(BF16),
                      "lambda_init": 0.8 - 0.6 * math.exp(-0.3 * i)})
        else:
            w_gates = jnp.pad(od_w_in[j][:, ML_MAIN:], ((0, 0), (0, LANES - ML_GATES)))
            p.update({"w_main": od_w_in[j][:, :ML_MAIN].astype(BF16), "w_gates": w_gates.astype(BF16),
                      "bias_col": od_gate_b[j].reshape(ML_GATES, 1),
                      "bias_row": od_gate_b[j].reshape(1, ML_GATES),
                      "out_norm": od_out_norm[j].reshape(1, D_MODEL),
                      "w_out": od_w_out[j].astype(BF16)})
        layers.append(p)

    outs = []
    for x3 in (x_prompt, x_sample):
        b, s, d = x3.shape
        x = x3.reshape(b * s, d)
        for i, p in enumerate(layers):
            x = _even_layer(x, b, s, p) if i % 2 == 0 else _odd_layer(x, b, s, p)
            x = ffn(x, p["norm_ffn"], p["wg"], p["wu"], p["wd"])
        outs.append(x.reshape(b, s, d))
    return tuple(outs)
```

```python
import functools
import math

import jax
import jax.numpy as jnp
from jax import lax
from jax.experimental import pallas as pl
from jax.experimental.pallas import tpu as pltpu

F32 = jnp.float32
BF16 = jnp.bfloat16

D_MODEL = 2048
CONV_CH = D_MODEL // 2
CONV_WIDTH = 31
CONV_PAD = CONV_WIDTH // 2
CONV_HALO = 16
DA_HEAD_DIM = 128
DA_HEADS = D_MODEL // (4 * DA_HEAD_DIM)
DA_QK = DA_HEADS * 2 * DA_HEAD_DIM
ML_HEADS = 4
ML_DV = D_MODEL // ML_HEADS
ML_DK = ML_DV // 2
ML_GATES = 4 * ML_HEADS
ML_MAIN = 2 * ML_HEADS * ML_DK + 2 * ML_HEADS * ML_DV
CHUNK = 128
ROPE_THETA = 10000.0
EPS = 1e-6
LANES = 128
SUBLANES = 8
LOG2E = math.log2(math.e)
SAFE_LOGIT_LOG2 = 64.0
VMEM_LIMIT = 56 * 1024 * 1024


def _params(semantics, vmem_limit=VMEM_LIMIT):
    return pltpu.CompilerParams(dimension_semantics=semantics, vmem_limit_bytes=vmem_limit)


def _rms(x, g):
    ms = jnp.mean(x * x, axis=-1, keepdims=True)
    return x * lax.rsqrt(ms + EPS) * g


def _norm_matmul_kernel(x_ref, g_ref, w_ref, o_ref, xn_ref, w_side_ref=None, o_side_ref=None):
    @pl.when(pl.program_id(1) == 0)
    def _():
        xn = _rms(x_ref[...], g_ref[...]).astype(BF16)
        xn_ref[...] = xn
        if w_side_ref is not None:
            o_side_ref[...] = jnp.dot(xn, w_side_ref[...], preferred_element_type=F32)

    o_ref[...] = jnp.dot(xn_ref[...], w_ref[...], preferred_element_type=F32).astype(o_ref.dtype)


def _norm_matmul_side_kernel(x_ref, g_ref, w_ref, w_side_ref, o_ref, o_side_ref, xn_ref):
    _norm_matmul_kernel(x_ref, g_ref, w_ref, o_ref, xn_ref, w_side_ref, o_side_ref)


def norm_matmul(x, g, w, w_side=None, tm=1024, tn=1024):
    m, d = x.shape
    n = w.shape[1]
    tm, tn = min(tm, m), min(tn, n)
    assert m % tm == 0 and n % tn == 0, (m, n, tm, tn)
    in_specs = [pl.BlockSpec((tm, d), lambda i, j: (i, 0)),
                pl.BlockSpec((1, d), lambda i, j: (0, 0)),
                pl.BlockSpec((d, tn), lambda i, j: (0, j))]
    out_specs = [pl.BlockSpec((tm, tn), lambda i, j: (i, j))]
    out_shape = [jax.ShapeDtypeStruct((m, n), BF16)]
    operands = [x, g, w]
    if w_side is not None:
        ns = w_side.shape[1]
        in_specs.append(pl.BlockSpec((d, ns), lambda i, j: (0, 0)))
        out_specs.append(pl.BlockSpec((tm, ns), lambda i, j: (i, 0)))
        out_shape.append(jax.ShapeDtypeStruct((m, ns), F32))
        operands.append(w_side)
    outs = pl.pallas_call(
        _norm_matmul_kernel if w_side is None else _norm_matmul_side_kernel,
        grid=(m // tm, n // tn),
        in_specs=in_specs,
        out_specs=out_specs,
        out_shape=out_shape,
        scratch_shapes=[pltpu.VMEM((tm, d), BF16)],
        compiler_params=_params(("parallel", "arbitrary")),
        name="norm_matmul",
    )(*operands)
    return outs[0] if w_side is None else outs


def _conv_kernel(ac_ref, ap_ref, an_ref, gc_ref, gp_ref, gn_ref, w_ref, b_ref, nrm_ref, o_ref, u_ref,
                 *, ts, tr):
    i = pl.program_id(1)
    n = pl.num_programs(1)

    def glu(a_ref, g_ref):
        return a_ref[0].astype(F32) * jax.nn.sigmoid(g_ref[0].astype(F32))

    u_ref[0, 0:CONV_HALO, :] = jnp.where(i > 0, glu(ap_ref, gp_ref), 0.0)
    u_ref[0, CONV_HALO:CONV_HALO + ts, :] = glu(ac_ref, gc_ref)
    u_ref[0, CONV_HALO + ts:, :] = jnp.where(i < n - 1, glu(an_ref, gn_ref), 0.0)
    rows = ts + 2 * CONV_HALO - SUBLANES
    for sh in range(1, SUBLANES):
        u_ref[sh, 0:rows, :] = u_ref[0, sh:sh + rows, :]

    bias = b_ref[...]
    nrm = nrm_ref[...]
    base = CONV_HALO - CONV_PAD
    for r in range(ts // tr):
        acc = jnp.zeros((tr, CONV_CH), F32)
        for k in range(CONV_WIDTH):
            sh, lo = (base + k) % SUBLANES, r * tr + (base + k) // SUBLANES * SUBLANES
            w_tap = w_ref[k * SUBLANES:(k + 1) * SUBLANES, :]
            acc = acc + u_ref[sh, lo:lo + tr, :] * jnp.concatenate([w_tap] * (tr // SUBLANES), axis=0)
        y = _rms(acc + bias, nrm)
        o_ref[0, r * tr:(r + 1) * tr, :] = (y * jax.nn.sigmoid(y)).astype(o_ref.dtype)


def conv_module(hcat, conv_w, conv_b, conv_norm, ts=256, tr=32):
    b, s, _ = hcat.shape
    assert s % ts == 0 and ts % tr == 0 and ts % CONV_HALO == 0, (s, ts, tr)
    nh = ts // CONV_HALO
    last = s // CONV_HALO - 1
    cur = lambda col: pl.BlockSpec((1, ts, CONV_CH), lambda bi, i: (bi, i, col))
    prev = lambda col: pl.BlockSpec((1, CONV_HALO, CONV_CH),
                                    lambda bi, i: (bi, jnp.maximum(i * nh - 1, 0), col))
    nxt = lambda col: pl.BlockSpec((1, CONV_HALO, CONV_CH),
                                   lambda bi, i: (bi, jnp.minimum((i + 1) * nh, last), col))
    vec = lambda rows: pl.BlockSpec((rows, CONV_CH), lambda bi, i: (0, 0))
    return pl.pallas_call(
        functools.partial(_conv_kernel, ts=ts, tr=tr),
        grid=(b, s // ts),
        in_specs=[cur(0), prev(0), nxt(0), cur(1), prev(1), nxt(1), vec(CONV_WIDTH * SUBLANES), vec(1), vec(1)],
        out_specs=pl.BlockSpec((1, ts, CONV_CH), lambda bi, i: (bi, i, 0)),
        out_shape=jax.ShapeDtypeStruct((b, s, CONV_CH), BF16),
        scratch_shapes=[pltpu.VMEM((SUBLANES, ts + 2 * CONV_HALO, CONV_CH), F32)],
        compiler_params=_params(("parallel", "arbitrary")),
        name="conv_module",
    )(hcat, hcat, hcat, hcat, hcat, hcat, jnp.repeat(conv_w, SUBLANES, axis=0), conv_b, conv_norm)


def _even_in_kernel(x_ref, g_ref, w_ref, cos_ref, sin_ref, qn_ref, kn_ref, o_ref, xn_ref,
                    *, n_steps, q_scale, rh):
    j = pl.program_id(1)

    @pl.when(j == 0)
    def _():
        xn_ref[...] = _rms(x_ref[...], g_ref[...]).astype(BF16)

    d = DA_HEAD_DIM
    tm, tn = o_ref.shape
    q_lo, k_lo, v_lo = 2 * CONV_CH, 2 * CONV_CH + DA_QK, 2 * CONV_CH + 2 * DA_QK

    def norm_rope(y, gain, mul, rows):
        yn = _rms(y, gain)
        rot = pltpu.roll(yn, d // 2, axis=1)
        return (yn * cos_ref[rows, :] + rot * sin_ref[rows, :]) * mul

    def finish(y, col, c0, rows):
        if q_lo <= col < v_lo:
            gain, mul = (qn_ref[...], q_scale) if col < k_lo else (kn_ref[...], 1.0)
            y = jnp.concatenate([norm_rope(y[:, :d], gain, mul, rows), norm_rope(y[:, d:], gain, mul, rows)],
                                axis=1)
        o_ref[rows, c0:c0 + 2 * d] = y.astype(o_ref.dtype)

    for step in range(n_steps):
        @pl.when(j == step)
        def _(step=step):
            pending = None
            for c0 in range(0, tn, 2 * d):
                for r0 in range(0, tm, rh):
                    rows = slice(r0, r0 + rh)
                    y = jnp.dot(xn_ref[rows, :], w_ref[:, c0:c0 + 2 * d], preferred_element_type=F32)
                    if pending is not None:
                        finish(*pending)
                    pending = (y, step * tn + c0, c0, rows)
            finish(*pending)


def even_in_proj(x, g, w, cos, sin, q_norm, k_norm, tm=1024, n_steps=2, rh=512):
    m, dm = x.shape
    n = w.shape[1]
    s = cos.shape[0]
    tm = min(tm, s)
    rh = min(rh, tm)
    tn = n // n_steps
    assert m % tm == 0 and s % tm == 0 and tm % rh == 0, (m, s, tm, rh)
    assert n % n_steps == 0 and tn % (2 * DA_HEAD_DIM) == 0, (n, n_steps)
    tab = pl.BlockSpec((tm, DA_HEAD_DIM), lambda i, j: (i % (s // tm), 0))
    vec = pl.BlockSpec((1, DA_HEAD_DIM), lambda i, j: (0, 0))
    return pl.pallas_call(
        functools.partial(_even_in_kernel, n_steps=n_steps, q_scale=LOG2E / math.sqrt(DA_HEAD_DIM), rh=rh),
        grid=(m // tm, n_steps),
        in_specs=[pl.BlockSpec((tm, dm), lambda i, j: (i, 0)),
                  pl.BlockSpec((1, dm), lambda i, j: (0, 0)),
                  pl.BlockSpec((dm, tn), lambda i, j: (0, j)),
                  tab, tab, vec, vec],
        out_specs=pl.BlockSpec((tm, tn), lambda i, j: (i, j)),
        out_shape=jax.ShapeDtypeStruct((m, n), BF16),
        scratch_shapes=[pltpu.VMEM((tm, dm), BF16)],
        compiler_params=_params(("parallel", "arbitrary")),
        name="even_in_proj",
    )(x, g, w, cos, sin, q_norm, k_norm)


def _attn_kernel(q_ref, k_ref, v_ref, lam_ref, subln_ref, o_ref, m_ref, l_ref, acc_ref,
                 *, lambda_init, stabilize):
    ki = pl.program_id(3)
    d = DA_HEAD_DIM

    @pl.when(ki == 0)
    def _():
        if stabilize:
            m_ref[...] = jnp.full_like(m_ref, -jnp.inf)
        l_ref[...] = jnp.zeros_like(l_ref)
        acc_ref[...] = jnp.zeros_like(acc_ref)

    v = v_ref[0]
    for c in range(2):
        q = q_ref[0, :, c * d:(c + 1) * d]
        k = k_ref[0, :, c * d:(c + 1) * d]
        s = lax.dot_general(q, k, (((1,), (1,)), ((), ())), preferred_element_type=F32)
        if stabilize:
            m_prev = m_ref[c]
            m_new = jnp.maximum(m_prev, jnp.max(s, axis=-1, keepdims=True))
            alpha = jnp.exp2(m_prev - m_new)
            m_ref[c] = m_new
            p = jnp.exp2(s - m_new)
        else:
            p = jnp.exp2(s)
        psum = p[:, 0:LANES]
        for j in range(1, p.shape[1] // LANES):
            psum = psum + p[:, j * LANES:(j + 1) * LANES]
        pv = jnp.dot(p.astype(BF16), v, preferred_element_type=F32)
        if stabilize:
            l_ref[c] = alpha * l_ref[c] + psum
            acc_ref[c] = alpha * acc_ref[c] + pv
        else:
            l_ref[c] += psum
            acc_ref[c] += pv

    @pl.when(ki == pl.num_programs(3) - 1)
    def _():
        lp = lam_ref[...]
        lam = (jnp.exp(jnp.sum(lp[0:1] * lp[1:2], axis=-1, keepdims=True))
               - jnp.exp(jnp.sum(lp[2:3] * lp[3:4], axis=-1, keepdims=True)) + lambda_init)
        l0 = jnp.sum(l_ref[0], axis=-1, keepdims=True)
        l1 = jnp.sum(l_ref[1], axis=-1, keepdims=True)
        o = acc_ref[0] / l0 - lam * (acc_ref[1] / l1)
        o_ref[0] = (_rms(o, subln_ref[...]) * (1.0 - lambda_init)).astype(o_ref.dtype)


def logit_bound_log2(q_norm, k_norm):
    return (LOG2E * math.sqrt(DA_HEAD_DIM)) * jnp.max(jnp.abs(q_norm)) * jnp.max(jnp.abs(k_norm))


def diff_attention(hcat, lam_params, subln, lambda_init, stabilize, tq=1024, tk=4096):
    b, s, n = hcat.shape
    tq, tk = min(tq, s), min(tk, s)
    assert s % tq == 0 and s % tk == 0, (s, tq, tk)
    dv = 2 * DA_HEAD_DIM
    q_col0, k_col0, v_col0 = [(n - r * DA_QK) // dv for r in (3, 2, 1)]
    return pl.pallas_call(
        functools.partial(_attn_kernel, lambda_init=lambda_init, stabilize=stabilize),
        grid=(b, DA_HEADS, s // tq, s // tk),
        in_specs=[pl.BlockSpec((1, tq, dv), lambda bi, h, qi, ki: (bi, qi, q_col0 + h)),
                  pl.BlockSpec((1, tk, dv), lambda bi, h, qi, ki: (bi, ki, k_col0 + h)),
                  pl.BlockSpec((1, tk, dv), lambda bi, h, qi, ki: (bi, ki, v_col0 + h)),
                  pl.BlockSpec((4, DA_HEAD_DIM), lambda bi, h, qi, ki: (0, 0)),
                  pl.BlockSpec((1, dv), lambda bi, h, qi, ki: (0, 0))],
        out_specs=pl.BlockSpec((1, tq, dv), lambda bi, h, qi, ki: (bi, qi, h)),
        out_shape=jax.ShapeDtypeStruct((b, s, DA_QK), BF16),
        scratch_shapes=[pltpu.VMEM((2, tq, 1), F32), pltpu.VMEM((2, tq, LANES), F32),
                        pltpu.VMEM((2, tq, dv), F32)],
        compiler_params=_params(("parallel", "parallel", "parallel", "arbitrary")),
        name="diff_attention",
    )(hcat, hcat, hcat, lam_params, subln)


def _even_out_kernel(x_ref, a_ref, b_ref, w_ref, o_ref):
    ka = a_ref.shape[1]
    y = jnp.dot(a_ref[...], w_ref[0:ka, :], preferred_element_type=F32)
    y = y + jnp.dot(b_ref[...], w_ref[ka:, :], preferred_element_type=F32)
    o_ref[...] = x_ref[...] + y


def even_out(x, y_conv, y_att, w, tm=512):
    m, d = x.shape
    assert m % tm == 0, (m, tm)
    row = lambda width: pl.BlockSpec((tm, width), lambda i: (i, 0))
    return pl.pallas_call(
        _even_out_kernel,
        grid=(m // tm,),
        in_specs=[row(d), row(y_conv.shape[1]), row(y_att.shape[1]),
                  pl.BlockSpec(w.shape, lambda i: (0, 0))],
        out_specs=row(d),
        out_shape=jax.ShapeDtypeStruct((m, d), F32),
        compiler_params=_params(("parallel",)),
        name="even_out",
    )(x, y_conv, y_att, w)


def _odd_out_kernel(x_ref, hf_ref, hb_ref, og_ref, nrm_ref, w_ref, o_ref, z_ref):
    for h in range(ML_HEADS):
        sl = slice(h * ML_DV, (h + 1) * ML_DV)
        hs = hf_ref[:, sl].astype(F32) + hb_ref[:, sl].astype(F32)
        z = _rms(hs, nrm_ref[:, sl]) * jax.nn.sigmoid(og_ref[:, sl].astype(F32))
        z_ref[:, sl] = z.astype(BF16)
    o_ref[...] = x_ref[...] + jnp.dot(z_ref[...], w_ref[...], preferred_element_type=F32)


def odd_out(x, h_fwd, h_bwd, hcat, out_norm, w, tm=512):
    m, d = x.shape
    assert m % tm == 0, (m, tm)
    row = lambda c: pl.BlockSpec((tm, d), lambda i: (i, c))
    return pl.pallas_call(
        _odd_out_kernel,
        grid=(m // tm,),
        in_specs=[row(0), row(0), row(0), row(2),
                  pl.BlockSpec((1, d), lambda i: (0, 0)),
                  pl.BlockSpec(w.shape, lambda i: (0, 0))],
        out_specs=row(0),
        out_shape=jax.ShapeDtypeStruct((m, d), F32),
        scratch_shapes=[pltpu.VMEM((tm, d), BF16)],
        compiler_params=_params(("parallel",)),
        name="odd_out",
    )(x, h_fwd, h_bwd, hcat, out_norm, w)


def _ffn_kernel(x_ref, g_ref, wg_ref, wu_ref, wd_ref, o_ref, xn_ref):
    j = pl.program_id(1)

    @pl.when(j == 0)
    def _():
        x = x_ref[...]
        xn_ref[...] = _rms(x, g_ref[...]).astype(BF16)
        o_ref[...] = x

    xn = xn_ref[...]
    gate = jnp.dot(xn, wg_ref[...], preferred_element_type=F32)
    up = jnp.dot(xn, wu_ref[...], preferred_element_type=F32)
    hid = (gate * jax.nn.sigmoid(gate) * up).astype(BF16)
    o_ref[...] += jnp.dot(hid, wd_ref[...], preferred_element_type=F32)


def ffn(x, g, wg, wu, wd, tm=1024, tf=512):
    m, d = x.shape
    f = wg.shape[1]
    tm = min(tm, m)
    assert m % tm == 0 and f % tf == 0, (m, f, tm, tf)
    vmem = 2 * 2 * tm * d * 4 + tm * d * 2 + 2 * 3 * d * tf * 2 + 3 * tm * tf * 4
    return pl.pallas_call(
        _ffn_kernel,
        grid=(m // tm, f // tf),
        in_specs=[pl.BlockSpec((tm, d), lambda i, j: (i, 0)),
                  pl.BlockSpec((1, d), lambda i, j: (0, 0)),
                  pl.BlockSpec((d, tf), lambda i, j: (0, j)),
                  pl.BlockSpec((d, tf), lambda i, j: (0, j)),
                  pl.BlockSpec((tf, d), lambda i, j: (j, 0))],
        out_specs=pl.BlockSpec((tm, d), lambda i, j: (i, 0)),
        out_shape=jax.ShapeDtypeStruct((m, d), F32),
        scratch_shapes=[pltpu.VMEM((tm, d), BF16)],
        compiler_params=_params(("parallel", "arbitrary"), vmem_limit=max(VMEM_LIMIT, vmem)),
        name="ffn",
    )(x, g, wg, wu, wd)


def _split3(x):
    x1 = x.astype(BF16)
    r = x - x1.astype(F32)
    x2 = r.astype(BF16)
    x3 = (r - x2.astype(F32)).astype(BF16)
    return x1, x2, x3


def _mlstm_kernel(qf_ref, ktf_ref, vf_ref, gf_ref, gtf_ref, qb_ref, ktb_ref, vb_ref, gb_ref, gtb_ref,
                  bc_ref, br_ref, hf_ref, hb_ref, c_ref, m_ref):
    @pl.when(pl.program_id(1) == 0)
    def _():
        c_ref[...] = jnp.zeros_like(c_ref)
        m_ref[...] = jnp.zeros_like(m_ref)

    _mlstm_direction(qf_ref, ktf_ref, vf_ref, gf_ref, gtf_ref, bc_ref, br_ref, hf_ref,
                     c_ref.at[0], m_ref.at[0], reverse=False)
    _mlstm_direction(qb_ref, ktb_ref, vb_ref, gb_ref, gtb_ref, bc_ref, br_ref, hb_ref,
                     c_ref.at[1], m_ref.at[1], reverse=True)


def _mlstm_direction(q_ref, kt_ref, v_ref, g_ref, gt_ref, bc_ref, br_ref, h_ref, c_ref, m_ref,
                     *, reverse):
    L = CHUNK
    t_idx = lax.broadcasted_iota(jnp.int32, (L, L), 0)
    s_idx = lax.broadcasted_iota(jnp.int32, (L, L), 1)
    mask = (s_idx >= t_idx) if reverse else (s_idx <= t_idx)
    mask_t = (s_idx <= t_idx) if reverse else (s_idx >= t_idx)
    tri_col = jnp.where(mask, 1.0, 0.0).astype(BF16)
    tri_row = jnp.where(mask_t, 1.0, 0.0).astype(BF16)
    last = 0 if reverse else L - 1

    g_col = g_ref[0][:, 0:ML_GATES] + br_ref[...]
    g_row = gt_ref[0] + bc_ref[...]
    lf_col = jax.nn.log_sigmoid(g_col)
    lf_row = jax.nn.log_sigmoid(g_row)
    bcum_col = sum(jnp.dot(tri_col, p, preferred_element_type=F32) for p in _split3(lf_col))
    bcum_row = sum(jnp.dot(p, tri_row, preferred_element_type=F32) for p in _split3(lf_row))

    it = 2 if reverse else 0
    for h in range(ML_HEADS):
        ci = it * ML_HEADS + h
        cf = (it + 1) * ML_HEADS + h
        i_row = g_row[ci:ci + 1, :]
        b_col = bcum_col[:, cf:cf + 1]
        b_row = bcum_row[cf:cf + 1, :]
        total = b_col[last:last + 1, :]
        m_prev = m_ref[h]

        q = q_ref[0, :, h * ML_DK:(h + 1) * ML_DK]
        kt = kt_ref[0, h * ML_DK:(h + 1) * ML_DK, :]
        v_aug = jnp.concatenate([v_ref[0, :, h * ML_DV:(h + 1) * ML_DV], jnp.ones((L, LANES), BF16)], axis=1)

        dlog = jnp.where(mask, b_col - b_row + i_row, -jnp.inf)
        inter = b_col + m_prev
        mt = jnp.maximum(inter, jnp.max(dlog, axis=-1, keepdims=True))
        dmat = jnp.exp(dlog - mt)
        qk = jnp.dot(q, kt, preferred_element_type=F32)
        sc = qk * dmat * (1.0 / math.sqrt(ML_DK))
        w_int = jnp.exp(inter - mt)
        qc = jnp.dot(q, c_ref[h].astype(BF16), preferred_element_type=F32)
        full = jnp.dot(sc.astype(BF16), v_aug, preferred_element_type=F32) + w_int * qc
        den = full[:, ML_DV:]
        inv = 1.0 / jnp.maximum(jnp.abs(den), jnp.exp(-mt))
        hout = full[:, :ML_DV] * jnp.concatenate([inv] * (ML_DV // LANES), axis=1)
        h_ref[0, :, h * ML_DV:(h + 1) * ML_DV] = hout.astype(h_ref.dtype)

        m_new = mt[last:last + 1, :]
        w_s = jnp.exp(total - b_row + i_row - m_new) * (1.0 / math.sqrt(ML_DK))
        decay = jnp.exp(total + m_prev - m_new)
        kwt = (kt.astype(F32) * w_s).astype(BF16)
        c_ref[h] = decay * c_ref[h] + jnp.dot(kwt, v_aug, preferred_element_type=F32)
        m_ref[h] = m_new


def mlstm_scan(hcat, k_t, gates, gates_t, bias_col, bias_row):
    b, s, _ = hcat.shape
    assert s % CHUNK == 0, (s, CHUNK)
    nc = s // CHUNK
    nq = ML_HEADS * ML_DK
    nv = ML_HEADS * ML_DV
    fwd = lambda c: c
    bwd = lambda c: nc - 1 - c

    def direction_specs(pos):
        return [pl.BlockSpec((1, CHUNK, nq), lambda bi, c: (bi, pos(c), 0)),
                pl.BlockSpec((1, nq, CHUNK), lambda bi, c: (bi, 0, pos(c))),
                pl.BlockSpec((1, CHUNK, nv), lambda bi, c: (bi, pos(c), 1)),
                pl.BlockSpec((1, CHUNK, LANES), lambda bi, c: (bi, pos(c), 0)),
                pl.BlockSpec((1, ML_GATES, CHUNK), lambda bi, c: (bi, 0, pos(c)))]

    out_spec = lambda pos: pl.BlockSpec((1, CHUNK, nv), lambda bi, c: (bi, pos(c), 0))
    operands = (hcat, k_t, hcat, gates, gates_t)
    return pl.pallas_call(
        _mlstm_kernel,
        grid=(b, nc),
        in_specs=direction_specs(fwd) + direction_specs(bwd)
        + [pl.BlockSpec((ML_GATES, 1), lambda bi, c: (0, 0)),
           pl.BlockSpec((1, ML_GATES), lambda bi, c: (0, 0))],
        out_specs=[out_spec(fwd), out_spec(bwd)],
        out_shape=[jax.ShapeDtypeStruct((b, s, nv), BF16)] * 2,
        scratch_shapes=[pltpu.VMEM((2, ML_HEADS, ML_DK, ML_DV + LANES), F32),
                        pltpu.VMEM((2, ML_HEADS, 1, 1), F32)],
        compiler_params=_params(("parallel", "arbitrary")),
        name="mlstm",
    )(*operands, *operands, bias_col, bias_row)


def _rope_tables(s):
    d = DA_HEAD_DIM
    inv = 1.0 / (ROPE_THETA ** (jnp.arange(0, d, 2, dtype=F32) / d))
    ang = jnp.arange(s, dtype=F32)[:, None] * inv[None, :]
    cos, sin = jnp.cos(ang), jnp.sin(ang)
    return jnp.concatenate([cos, cos], axis=-1), jnp.concatenate([-sin, sin], axis=-1)


def _even_layer(x, b, s, p):
    m = b * s
    cos, sin = _rope_tables(s)
    hcat = even_in_proj(x, p["norm_mix"], p["w_in"], cos, sin, p["q_norm"], p["k_norm"]).reshape(b, s, -1)
    y_conv = conv_module(hcat, p["conv_w"], p["conv_b"], p["conv_norm"])
    attn = lambda stabilize: functools.partial(diff_attention, lam_params=p["lam"], subln=p["subln"],
                                               lambda_init=p["lambda_init"], stabilize=stabilize)
    y_att = lax.cond(logit_bound_log2(p["q_norm"], p["k_norm"]) < SAFE_LOGIT_LOG2,
                     attn(False), attn(True), hcat)
    return even_out(x, y_conv.reshape(m, -1), y_att.reshape(m, -1), p["w_out"])


def _odd_layer(x, b, s, p):
    m = b * s
    hcat, gates = norm_matmul(x, p["norm_mix"], p["w_main"], w_side=p["w_gates"], tn=p["w_main"].shape[1] // 3)
    gates = gates.reshape(b, s, LANES)
    gates_t = jnp.swapaxes(gates[:, :, :ML_GATES], 1, 2)
    hcat3 = hcat.reshape(b, s, -1)
    nq = ML_HEADS * ML_DK
    k_t = jnp.swapaxes(hcat3[:, :, nq:2 * nq], 1, 2)
    h_fwd, h_bwd = mlstm_scan(hcat3, k_t, gates, gates_t, p["bias_col"], p["bias_row"])
    return odd_out(x, h_fwd.reshape(m, -1), h_bwd.reshape(m, -1), hcat, p["out_norm"], p["w_out"])


def kernel(x_prompt, x_sample, norm_mix, norm_ffn, ev_w_in, ev_conv_w, ev_conv_b, ev_conv_norm, ev_q_norm, ev_k_norm, ev_lambda, ev_subln, ev_w_out, od_w_in, od_gate_b, od_out_norm, od_w_out, ffn_w_gate, ffn_w_up, ffn_w_down):
    depth = norm_mix.shape[0]
    layers = []
    for i in range(depth):
        j = i // 2
        p = {"norm_mix": norm_mix[i][None], "norm_ffn": norm_ffn[i][None],
             "wg": ffn_w_gate[i].astype(BF16), "wu": ffn_w_up[i].astype(BF16),
             "wd": ffn_w_down[i].astype(BF16)}
        if i % 2 == 0:
            p.update({"w_in": ev_w_in[j].astype(BF16), "conv_w": ev_conv_w[j], "conv_b": ev_conv_b[j][None],
                      "conv_norm": ev_conv_norm[j][None], "q_norm": ev_q_norm[j][None],
                      "k_norm": ev_k_norm[j][None], "lam": ev_lambda[j], "subln": ev_subln[j][None],
                      "w_out": ev_w_out[j].astype(BF16),
                      "lambda_init": 0.8 - 0.6 * math.exp(-0.3 * i)})
        else:
            w_gates = jnp.pad(od_w_in[j][:, ML_MAIN:], ((0, 0), (0, LANES - ML_GATES)))
            p.update({"w_main": od_w_in[j][:, :ML_MAIN].astype(BF16), "w_gates": w_gates.astype(BF16),
                      "bias_col": od_gate_b[j].reshape(ML_GATES, 1),
                      "bias_row": od_gate_b[j].reshape(1, ML_GATES),
                      "out_norm": od_out_norm[j].reshape(1, D_MODEL),
                      "w_out": od_w_out[j].astype(BF16)})
        layers.append(p)

    outs = []
    for x3 in (x_prompt, x_sample):
        b, s, d = x3.shape
        x = x3.reshape(b * s, d)
        for i, p in enumerate(layers):
            x = _even_layer(x, b, s, p) if i % 2 == 0 else _odd_layer(x, b, s, p)
            x = ffn(x, p["norm_ffn"], p["wg"], p["wu"], p["wd"])
        outs.append(x.reshape(b, s, d))
    return tuple(outs)
```

```python
import functools
import math

import jax
import jax.numpy as jnp
from jax import lax
from jax.experimental import pallas as pl
from jax.experimental.pallas import tpu as pltpu

F32 = jnp.float32
BF16 = jnp.bfloat16

D_MODEL = 2048
CONV_CH = D_MODEL // 2
CONV_WIDTH = 31
CONV_PAD = CONV_WIDTH // 2
CONV_HALO = 16
DA_HEAD_DIM = 128
DA_HEADS = D_MODEL // (4 * DA_HEAD_DIM)
DA_QK = DA_HEADS * 2 * DA_HEAD_DIM
ML_HEADS = 4
ML_DV = D_MODEL // ML_HEADS
ML_DK = ML_DV // 2
ML_GATES = 4 * ML_HEADS
ML_MAIN = 2 * ML_HEADS * ML_DK + 2 * ML_HEADS * ML_DV
CHUNK = 128
ROPE_THETA = 10000.0
EPS = 1e-6
LANES = 128
SUBLANES = 8
LOG2E = math.log2(math.e)
SAFE_LOGIT_LOG2 = 64.0
VMEM_LIMIT = 56 * 1024 * 1024


def _params(semantics, vmem_limit=VMEM_LIMIT):
    return pltpu.CompilerParams(dimension_semantics=semantics, vmem_limit_bytes=vmem_limit)


def _rms(x, g):
    ms = jnp.mean(x * x, axis=-1, keepdims=True)
    return x * lax.rsqrt(ms + EPS) * g


def _norm_matmul_kernel(x_ref, g_ref, w_ref, o_ref, xn_ref, w_side_ref=None, w_t_ref=None,
                        o_side_ref=None, o_t_ref=None):
    @pl.when(pl.program_id(1) == 0)
    def _():
        xn = _rms(x_ref[...], g_ref[...]).astype(BF16)
        xn_ref[...] = xn
        if w_side_ref is not None:
            o_side_ref[...] = jnp.dot(xn, w_side_ref[...], preferred_element_type=F32)
            o_t_ref[...] = lax.dot_general(w_t_ref[...], xn, (((1,), (1,)), ((), ())),
                                           preferred_element_type=F32).astype(o_t_ref.dtype)

    o_ref[...] = jnp.dot(xn_ref[...], w_ref[...], preferred_element_type=F32).astype(o_ref.dtype)


def _norm_matmul_side_kernel(x_ref, g_ref, w_ref, w_side_ref, w_t_ref, o_ref, o_side_ref, o_t_ref, xn_ref):
    _norm_matmul_kernel(x_ref, g_ref, w_ref, o_ref, xn_ref, w_side_ref, w_t_ref, o_side_ref, o_t_ref)


def norm_matmul(x, g, w, w_side=None, w_t=None, tm=1024, tn=1024):
    m, d = x.shape
    n = w.shape[1]
    tm, tn = min(tm, m), min(tn, n)
    assert m % tm == 0 and n % tn == 0, (m, n, tm, tn)
    in_specs = [pl.BlockSpec((tm, d), lambda i, j: (i, 0)),
                pl.BlockSpec((1, d), lambda i, j: (0, 0)),
                pl.BlockSpec((d, tn), lambda i, j: (0, j))]
    out_specs = [pl.BlockSpec((tm, tn), lambda i, j: (i, j))]
    out_shape = [jax.ShapeDtypeStruct((m, n), BF16)]
    operands = [x, g, w]
    if w_side is not None:
        ns, nt = w_side.shape[1], w_t.shape[0]
        in_specs += [pl.BlockSpec((d, ns), lambda i, j: (0, 0)), pl.BlockSpec((nt, d), lambda i, j: (0, 0))]
        out_specs += [pl.BlockSpec((tm, ns), lambda i, j: (i, 0)), pl.BlockSpec((nt, tm), lambda i, j: (0, i))]
        out_shape += [jax.ShapeDtypeStruct((m, ns), F32), jax.ShapeDtypeStruct((nt, m), BF16)]
        operands += [w_side, w_t]
    outs = pl.pallas_call(
        _norm_matmul_kernel if w_side is None else _norm_matmul_side_kernel,
        grid=(m // tm, n // tn),
        in_specs=in_specs,
        out_specs=out_specs,
        out_shape=out_shape,
        scratch_shapes=[pltpu.VMEM((tm, d), BF16)],
        compiler_params=_params(("parallel", "arbitrary")),
        name="norm_matmul",
    )(*operands)
    return outs[0] if w_side is None else outs


def _conv_kernel(ac_ref, ap_ref, an_ref, gc_ref, gp_ref, gn_ref, w_ref, b_ref, nrm_ref, o_ref, u_ref,
                 *, ts, tr):
    i = pl.program_id(1)
    n = pl.num_programs(1)

    def glu(a_ref, g_ref):
        return a_ref[0].astype(F32) * jax.nn.sigmoid(g_ref[0].astype(F32))

    u_ref[0, 0:CONV_HALO, :] = jnp.where(i > 0, glu(ap_ref, gp_ref), 0.0)
    u_ref[0, CONV_HALO:CONV_HALO + ts, :] = glu(ac_ref, gc_ref)
    u_ref[0, CONV_HALO + ts:, :] = jnp.where(i < n - 1, glu(an_ref, gn_ref), 0.0)
    rows = ts + 2 * CONV_HALO - SUBLANES
    for sh in range(1, SUBLANES):
        u_ref[sh, 0:rows, :] = u_ref[0, sh:sh + rows, :]

    bias = b_ref[...]
    nrm = nrm_ref[...]
    base = CONV_HALO - CONV_PAD
    for r in range(ts // tr):
        acc = jnp.zeros((tr, CONV_CH), F32)
        for k in range(CONV_WIDTH):
            sh, lo = (base + k) % SUBLANES, r * tr + (base + k) // SUBLANES * SUBLANES
            w_tap = w_ref[k * SUBLANES:(k + 1) * SUBLANES, :]
            acc = acc + u_ref[sh, lo:lo + tr, :] * jnp.concatenate([w_tap] * (tr // SUBLANES), axis=0)
        y = _rms(acc + bias, nrm)
        o_ref[0, r * tr:(r + 1) * tr, :] = (y * jax.nn.sigmoid(y)).astype(o_ref.dtype)


def conv_module(hcat, conv_w, conv_b, conv_norm, ts=256, tr=32):
    b, s, _ = hcat.shape
    assert s % ts == 0 and ts % tr == 0 and ts % CONV_HALO == 0, (s, ts, tr)
    nh = ts // CONV_HALO
    last = s // CONV_HALO - 1
    cur = lambda col: pl.BlockSpec((1, ts, CONV_CH), lambda bi, i: (bi, i, col))
    prev = lambda col: pl.BlockSpec((1, CONV_HALO, CONV_CH),
                                    lambda bi, i: (bi, jnp.maximum(i * nh - 1, 0), col))
    nxt = lambda col: pl.BlockSpec((1, CONV_HALO, CONV_CH),
                                   lambda bi, i: (bi, jnp.minimum((i + 1) * nh, last), col))
    vec = lambda rows: pl.BlockSpec((rows, CONV_CH), lambda bi, i: (0, 0))
    return pl.pallas_call(
        functools.partial(_conv_kernel, ts=ts, tr=tr),
        grid=(b, s // ts),
        in_specs=[cur(0), prev(0), nxt(0), cur(1), prev(1), nxt(1), vec(CONV_WIDTH * SUBLANES), vec(1), vec(1)],
        out_specs=pl.BlockSpec((1, ts, CONV_CH), lambda bi, i: (bi, i, 0)),
        out_shape=jax.ShapeDtypeStruct((b, s, CONV_CH), BF16),
        scratch_shapes=[pltpu.VMEM((SUBLANES, ts + 2 * CONV_HALO, CONV_CH), F32)],
        compiler_params=_params(("parallel", "arbitrary")),
        name="conv_module",
    )(hcat, hcat, hcat, hcat, hcat, hcat, jnp.repeat(conv_w, SUBLANES, axis=0), conv_b, conv_norm)


def _even_in_kernel(x_ref, g_ref, w_ref, cos_ref, sin_ref, qn_ref, kn_ref, o_ref, xn_ref,
                    *, n_steps, q_scale, rh):
    j = pl.program_id(1)

    @pl.when(j == 0)
    def _():
        xn_ref[...] = _rms(x_ref[...], g_ref[...]).astype(BF16)

    d = DA_HEAD_DIM
    tm, tn = o_ref.shape
    q_lo, k_lo, v_lo = 2 * CONV_CH, 2 * CONV_CH + DA_QK, 2 * CONV_CH + 2 * DA_QK

    def norm_rope(y, gain, mul, rows):
        yn = _rms(y, gain)
        rot = pltpu.roll(yn, d // 2, axis=1)
        return (yn * cos_ref[rows, :] + rot * sin_ref[rows, :]) * mul

    def finish(y, col, c0, rows):
        if q_lo <= col < v_lo:
            gain, mul = (qn_ref[...], q_scale) if col < k_lo else (kn_ref[...], 1.0)
            y = jnp.concatenate([norm_rope(y[:, :d], gain, mul, rows), norm_rope(y[:, d:], gain, mul, rows)],
                                axis=1)
        o_ref[rows, c0:c0 + 2 * d] = y.astype(o_ref.dtype)

    for step in range(n_steps):
        @pl.when(j == step)
        def _(step=step):
            pending = None
            for c0 in range(0, tn, 2 * d):
                for r0 in range(0, tm, rh):
                    rows = slice(r0, r0 + rh)
                    y = jnp.dot(xn_ref[rows, :], w_ref[:, c0:c0 + 2 * d], preferred_element_type=F32)
                    if pending is not None:
                        finish(*pending)
                    pending = (y, step * tn + c0, c0, rows)
            finish(*pending)


def even_in_proj(x, g, w, cos, sin, q_norm, k_norm, tm=1024, n_steps=2, rh=512):
    m, dm = x.shape
    n = w.shape[1]
    s = cos.shape[0]
    tm = min(tm, s)
    rh = min(rh, tm)
    tn = n // n_steps
    assert m % tm == 0 and s % tm == 0 and tm % rh == 0, (m, s, tm, rh)
    assert n % n_steps == 0 and tn % (2 * DA_HEAD_DIM) == 0, (n, n_steps)
    tab = pl.BlockSpec((tm, DA_HEAD_DIM), lambda i, j: (i % (s // tm), 0))
    vec = pl.BlockSpec((1, DA_HEAD_DIM), lambda i, j: (0, 0))
    return pl.pallas_call(
        functools.partial(_even_in_kernel, n_steps=n_steps, q_scale=LOG2E / math.sqrt(DA_HEAD_DIM), rh=rh),
        grid=(m // tm, n_steps),
        in_specs=[pl.BlockSpec((tm, dm), lambda i, j: (i, 0)),
                  pl.BlockSpec((1, dm), lambda i, j: (0, 0)),
                  pl.BlockSpec((dm, tn), lambda i, j: (0, j)),
                  tab, tab, vec, vec],
        out_specs=pl.BlockSpec((tm, tn), lambda i, j: (i, j)),
        out_shape=jax.ShapeDtypeStruct((m, n), BF16),
        scratch_shapes=[pltpu.VMEM((tm, dm), BF16)],
        compiler_params=_params(("parallel", "arbitrary")),
        name="even_in_proj",
    )(x, g, w, cos, sin, q_norm, k_norm)


def _attn_kernel(q_ref, k_ref, v_ref, lam_ref, subln_ref, o_ref, m_ref, l_ref, acc_ref,
                 *, lambda_init, stabilize):
    ki = pl.program_id(3)
    d = DA_HEAD_DIM

    @pl.when(ki == 0)
    def _():
        if stabilize:
            m_ref[...] = jnp.full_like(m_ref, -jnp.inf)
        l_ref[...] = jnp.zeros_like(l_ref)
        acc_ref[...] = jnp.zeros_like(acc_ref)

    v = v_ref[0]
    for c in range(2):
        q = q_ref[0, :, c * d:(c + 1) * d]
        k = k_ref[0, :, c * d:(c + 1) * d]
        s = lax.dot_general(q, k, (((1,), (1,)), ((), ())), preferred_element_type=F32)
        if stabilize:
            m_prev = m_ref[c]
            m_new = jnp.maximum(m_prev, jnp.max(s, axis=-1, keepdims=True))
            alpha = jnp.exp2(m_prev - m_new)
            m_ref[c] = m_new
            p = jnp.exp2(s - m_new)
        else:
            p = jnp.exp2(s)
        psum = p[:, 0:LANES]
        for j in range(1, p.shape[1] // LANES):
            psum = psum + p[:, j * LANES:(j + 1) * LANES]
        pv = jnp.dot(p.astype(BF16), v, preferred_element_type=F32)
        if stabilize:
            l_ref[c] = alpha * l_ref[c] + psum
            acc_ref[c] = alpha * acc_ref[c] + pv
        else:
            l_ref[c] += psum
            acc_ref[c] += pv

    @pl.when(ki == pl.num_programs(3) - 1)
    def _():
        lp = lam_ref[...]
        lam = (jnp.exp(jnp.sum(lp[0:1] * lp[1:2], axis=-1, keepdims=True))
               - jnp.exp(jnp.sum(lp[2:3] * lp[3:4], axis=-1, keepdims=True)) + lambda_init)
        l0 = jnp.sum(l_ref[0], axis=-1, keepdims=True)
        l1 = jnp.sum(l_ref[1], axis=-1, keepdims=True)
        o = acc_ref[0] / l0 - lam * (acc_ref[1] / l1)
        o_ref[0] = (_rms(o, subln_ref[...]) * (1.0 - lambda_init)).astype(o_ref.dtype)


def logit_bound_log2(q_norm, k_norm):
    return (LOG2E * math.sqrt(DA_HEAD_DIM)) * jnp.max(jnp.abs(q_norm)) * jnp.max(jnp.abs(k_norm))


def diff_attention(hcat, lam_params, subln, lambda_init, stabilize, tq=1024, tk=4096):
    b, s, n = hcat.shape
    tq, tk = min(tq, s), min(tk, s)
    assert s % tq == 0 and s % tk == 0, (s, tq, tk)
    dv = 2 * DA_HEAD_DIM
    q_col0, k_col0, v_col0 = [(n - r * DA_QK) // dv for r in (3, 2, 1)]
    return pl.pallas_call(
        functools.partial(_attn_kernel, lambda_init=lambda_init, stabilize=stabilize),
        grid=(b, DA_HEADS, s // tq, s // tk),
        in_specs=[pl.BlockSpec((1, tq, dv), lambda bi, h, qi, ki: (bi, qi, q_col0 + h)),
                  pl.BlockSpec((1, tk, dv), lambda bi, h, qi, ki: (bi, ki, k_col0 + h)),
                  pl.BlockSpec((1, tk, dv), lambda bi, h, qi, ki: (bi, ki, v_col0 + h)),
                  pl.BlockSpec((4, DA_HEAD_DIM), lambda bi, h, qi, ki: (0, 0)),
                  pl.BlockSpec((1, dv), lambda bi, h, qi, ki: (0, 0))],
        out_specs=pl.BlockSpec((1, tq, dv), lambda bi, h, qi, ki: (bi, qi, h)),
        out_shape=jax.ShapeDtypeStruct((b, s, DA_QK), BF16),
        scratch_shapes=[pltpu.VMEM((2, tq, 1), F32), pltpu.VMEM((2, tq, LANES), F32),
                        pltpu.VMEM((2, tq, dv), F32)],
        compiler_params=_params(("parallel", "parallel", "parallel", "arbitrary")),
        name="diff_attention",
    )(hcat, hcat, hcat, lam_params, subln)


def _even_out_kernel(x_ref, a_ref, b_ref, w_ref, o_ref):
    ka = a_ref.shape[1]
    y = jnp.dot(a_ref[...], w_ref[0:ka, :], preferred_element_type=F32)
    y = y + jnp.dot(b_ref[...], w_ref[ka:, :], preferred_element_type=F32)
    o_ref[...] = x_ref[...] + y


def even_out(x, y_conv, y_att, w, tm=512):
    m, d = x.shape
    assert m % tm == 0, (m, tm)
    row = lambda width: pl.BlockSpec((tm, width), lambda i: (i, 0))
    return pl.pallas_call(
        _even_out_kernel,
        grid=(m // tm,),
        in_specs=[row(d), row(y_conv.shape[1]), row(y_att.shape[1]),
                  pl.BlockSpec(w.shape, lambda i: (0, 0))],
        out_specs=row(d),
        out_shape=jax.ShapeDtypeStruct((m, d), F32),
        compiler_params=_params(("parallel",)),
        name="even_out",
    )(x, y_conv, y_att, w)


def _odd_out_kernel(x_ref, hf_ref, hb_ref, og_ref, nrm_ref, w_ref, o_ref, z_ref):
    for h in range(ML_HEADS):
        sl = slice(h * ML_DV, (h + 1) * ML_DV)
        hs = hf_ref[:, sl].astype(F32) + hb_ref[:, sl].astype(F32)
        z = _rms(hs, nrm_ref[:, sl]) * jax.nn.sigmoid(og_ref[:, sl].astype(F32))
        z_ref[:, sl] = z.astype(BF16)
    o_ref[...] = x_ref[...] + jnp.dot(z_ref[...], w_ref[...], preferred_element_type=F32)


def odd_out(x, h_fwd, h_bwd, hcat, out_norm, w, tm=512):
    m, d = x.shape
    assert m % tm == 0, (m, tm)
    row = lambda c: pl.BlockSpec((tm, d), lambda i: (i, c))
    return pl.pallas_call(
        _odd_out_kernel,
        grid=(m // tm,),
        in_specs=[row(0), row(0), row(0), row(1),
                  pl.BlockSpec((1, d), lambda i: (0, 0)),
                  pl.BlockSpec(w.shape, lambda i: (0, 0))],
        out_specs=row(0),
        out_shape=jax.ShapeDtypeStruct((m, d), F32),
        scratch_shapes=[pltpu.VMEM((tm, d), BF16)],
        compiler_params=_params(("parallel",)),
        name="odd_out",
    )(x, h_fwd, h_bwd, hcat, out_norm, w)


def _ffn_kernel(x_ref, g_ref, wg_ref, wu_ref, wd_ref, o_ref, xn_ref):
    j = pl.program_id(1)

    @pl.when(j == 0)
    def _():
        x = x_ref[...]
        xn_ref[...] = _rms(x, g_ref[...]).astype(BF16)
        o_ref[...] = x

    xn = xn_ref[...]
    gate = jnp.dot(xn, wg_ref[...], preferred_element_type=F32)
    up = jnp.dot(xn, wu_ref[...], preferred_element_type=F32)
    hid = (gate * jax.nn.sigmoid(gate) * up).astype(BF16)
    o_ref[...] += jnp.dot(hid, wd_ref[...], preferred_element_type=F32)


def ffn(x, g, wg, wu, wd, tm=1024, tf=512):
    m, d = x.shape
    f = wg.shape[1]
    tm = min(tm, m)
    assert m % tm == 0 and f % tf == 0, (m, f, tm, tf)
    vmem = 2 * 2 * tm * d * 4 + tm * d * 2 + 2 * 3 * d * tf * 2 + 3 * tm * tf * 4
    return pl.pallas_call(
        _ffn_kernel,
        grid=(m // tm, f // tf),
        in_specs=[pl.BlockSpec((tm, d), lambda i, j: (i, 0)),
                  pl.BlockSpec((1, d), lambda i, j: (0, 0)),
                  pl.BlockSpec((d, tf), lambda i, j: (0, j)),
                  pl.BlockSpec((d, tf), lambda i, j: (0, j)),
                  pl.BlockSpec((tf, d), lambda i, j: (j, 0))],
        out_specs=pl.BlockSpec((tm, d), lambda i, j: (i, 0)),
        out_shape=jax.ShapeDtypeStruct((m, d), F32),
        scratch_shapes=[pltpu.VMEM((tm, d), BF16)],
        compiler_params=_params(("parallel", "arbitrary"), vmem_limit=max(VMEM_LIMIT, vmem)),
        name="ffn",
    )(x, g, wg, wu, wd)


def _split3(x):
    x1 = x.astype(BF16)
    r = x - x1.astype(F32)
    x2 = r.astype(BF16)
    x3 = (r - x2.astype(F32)).astype(BF16)
    return x1, x2, x3


def _mlstm_kernel(qf_ref, ktf_ref, vf_ref, gf_ref, gtf_ref, qb_ref, ktb_ref, vb_ref, gb_ref, gtb_ref,
                  bc_ref, br_ref, hf_ref, hb_ref, c_ref, m_ref):
    @pl.when(pl.program_id(1) == 0)
    def _():
        c_ref[...] = jnp.zeros_like(c_ref)
        m_ref[...] = jnp.zeros_like(m_ref)

    _mlstm_direction(qf_ref, ktf_ref, vf_ref, gf_ref, gtf_ref, bc_ref, br_ref, hf_ref,
                     c_ref.at[0], m_ref.at[0], reverse=False)
    _mlstm_direction(qb_ref, ktb_ref, vb_ref, gb_ref, gtb_ref, bc_ref, br_ref, hb_ref,
                     c_ref.at[1], m_ref.at[1], reverse=True)


def _mlstm_direction(q_ref, kt_ref, v_ref, g_ref, gt_ref, bc_ref, br_ref, h_ref, c_ref, m_ref,
                     *, reverse):
    L = CHUNK
    t_idx = lax.broadcasted_iota(jnp.int32, (L, L), 0)
    s_idx = lax.broadcasted_iota(jnp.int32, (L, L), 1)
    mask = (s_idx >= t_idx) if reverse else (s_idx <= t_idx)
    mask_t = (s_idx <= t_idx) if reverse else (s_idx >= t_idx)
    tri_col = jnp.where(mask, 1.0, 0.0).astype(BF16)
    tri_row = jnp.where(mask_t, 1.0, 0.0).astype(BF16)
    last = 0 if reverse else L - 1

    g_col = g_ref[0][:, 0:ML_GATES] + br_ref[...]
    g_row = gt_ref[0] + bc_ref[...]
    lf_col = jax.nn.log_sigmoid(g_col)
    lf_row = jax.nn.log_sigmoid(g_row)
    bcum_col = sum(jnp.dot(tri_col, p, preferred_element_type=F32) for p in _split3(lf_col))
    bcum_row = sum(jnp.dot(p, tri_row, preferred_element_type=F32) for p in _split3(lf_row))

    it = 2 if reverse else 0
    for h in range(ML_HEADS):
        ci = it * ML_HEADS + h
        cf = (it + 1) * ML_HEADS + h
        i_row = g_row[ci:ci + 1, :]
        b_col = bcum_col[:, cf:cf + 1]
        b_row = bcum_row[cf:cf + 1, :]
        total = b_col[last:last + 1, :]
        m_prev = m_ref[h]

        q = q_ref[0, :, h * ML_DK:(h + 1) * ML_DK]
        kt = kt_ref[h * ML_DK:(h + 1) * ML_DK, :]
        v_aug = jnp.concatenate([v_ref[0, :, h * ML_DV:(h + 1) * ML_DV], jnp.ones((L, LANES), BF16)], axis=1)

        dlog = jnp.where(mask, b_col - b_row + i_row, -jnp.inf)
        inter = b_col + m_prev
        mt = jnp.maximum(inter, jnp.max(dlog, axis=-1, keepdims=True))
        dmat = jnp.exp(dlog - mt)
        qk = jnp.dot(q, kt, preferred_element_type=F32)
        sc = qk * dmat * (1.0 / math.sqrt(ML_DK))
        w_int = jnp.exp(inter - mt)
        qc = jnp.dot(q, c_ref[h].astype(BF16), preferred_element_type=F32)
        full = jnp.dot(sc.astype(BF16), v_aug, preferred_element_type=F32) + w_int * qc
        den = full[:, ML_DV:]
        inv = 1.0 / jnp.maximum(jnp.abs(den), jnp.exp(-mt))
        hout = full[:, :ML_DV] * jnp.concatenate([inv] * (ML_DV // LANES), axis=1)
        h_ref[0, :, h * ML_DV:(h + 1) * ML_DV] = hout.astype(h_ref.dtype)

        m_new = mt[last:last + 1, :]
        w_s = jnp.exp(total - b_row + i_row - m_new) * (1.0 / math.sqrt(ML_DK))
        decay = jnp.exp(total + m_prev - m_new)
        kwt = (kt.astype(F32) * w_s).astype(BF16)
        c_ref[h] = decay * c_ref[h] + jnp.dot(kwt, v_aug, preferred_element_type=F32)
        m_ref[h] = m_new


def mlstm_scan(hcat, k_t, gates, gates_t, bias_col, bias_row):
    b, s, _ = hcat.shape
    assert s % CHUNK == 0, (s, CHUNK)
    nc = s // CHUNK
    nq = ML_HEADS * ML_DK
    nv = ML_HEADS * ML_DV
    q_col = 2 * nv // nq
    fwd = lambda c: c
    bwd = lambda c: nc - 1 - c

    def direction_specs(pos):
        return [pl.BlockSpec((1, CHUNK, nq), lambda bi, c: (bi, pos(c), q_col)),
                pl.BlockSpec((nq, CHUNK), lambda bi, c: (0, bi * nc + pos(c))),
                pl.BlockSpec((1, CHUNK, nv), lambda bi, c: (bi, pos(c), 0)),
                pl.BlockSpec((1, CHUNK, LANES), lambda bi, c: (bi, pos(c), 0)),
                pl.BlockSpec((1, ML_GATES, CHUNK), lambda bi, c: (bi, 0, pos(c)))]

    out_spec = lambda pos: pl.BlockSpec((1, CHUNK, nv), lambda bi, c: (bi, pos(c), 0))
    operands = (hcat, k_t, hcat, gates, gates_t)
    return pl.pallas_call(
        _mlstm_kernel,
        grid=(b, nc),
        in_specs=direction_specs(fwd) + direction_specs(bwd)
        + [pl.BlockSpec((ML_GATES, 1), lambda bi, c: (0, 0)),
           pl.BlockSpec((1, ML_GATES), lambda bi, c: (0, 0))],
        out_specs=[out_spec(fwd), out_spec(bwd)],
        out_shape=[jax.ShapeDtypeStruct((b, s, nv), BF16)] * 2,
        scratch_shapes=[pltpu.VMEM((2, ML_HEADS, ML_DK, ML_DV + LANES), F32),
                        pltpu.VMEM((2, ML_HEADS, 1, 1), F32)],
        compiler_params=_params(("parallel", "arbitrary")),
        name="mlstm",
    )(*operands, *operands, bias_col, bias_row)


def _rope_tables(s):
    d = DA_HEAD_DIM
    inv = 1.0 / (ROPE_THETA ** (jnp.arange(0, d, 2, dtype=F32) / d))
    ang = jnp.arange(s, dtype=F32)[:, None] * inv[None, :]
    cos, sin = jnp.cos(ang), jnp.sin(ang)
    return jnp.concatenate([cos, cos], axis=-1), jnp.concatenate([-sin, sin], axis=-1)


def _even_layer(x, b, s, p):
    m = b * s
    cos, sin = _rope_tables(s)
    hcat = even_in_proj(x, p["norm_mix"], p["w_in"], cos, sin, p["q_norm"], p["k_norm"]).reshape(b, s, -1)
    y_conv = conv_module(hcat, p["conv_w"], p["conv_b"], p["conv_norm"])
    attn = lambda stabilize: functools.partial(diff_attention, lam_params=p["lam"], subln=p["subln"],
                                               lambda_init=p["lambda_init"], stabilize=stabilize)
    y_att = lax.cond(logit_bound_log2(p["q_norm"], p["k_norm"]) < SAFE_LOGIT_LOG2,
                     attn(False), attn(True), hcat)
    return even_out(x, y_conv.reshape(m, -1), y_att.reshape(m, -1), p["w_out"])


def _odd_layer(x, b, s, p):
    m = b * s
    hcat, gates, k_t = norm_matmul(x, p["norm_mix"], p["w_main"], w_side=p["w_gates"], w_t=p["w_kt"],
                                   tn=p["w_main"].shape[1] // 4)
    gates = gates.reshape(b, s, LANES)
    gates_t = jnp.swapaxes(gates[:, :, :ML_GATES], 1, 2)
    h_fwd, h_bwd = mlstm_scan(hcat.reshape(b, s, -1), k_t, gates, gates_t, p["bias_col"], p["bias_row"])
    return odd_out(x, h_fwd.reshape(m, -1), h_bwd.reshape(m, -1), hcat, p["out_norm"], p["w_out"])


def kernel(x_prompt, x_sample, norm_mix, norm_ffn, ev_w_in, ev_conv_w, ev_conv_b, ev_conv_norm, ev_q_norm, ev_k_norm, ev_lambda, ev_subln, ev_w_out, od_w_in, od_gate_b, od_out_norm, od_w_out, ffn_w_gate, ffn_w_up, ffn_w_down):
    depth = norm_mix.shape[0]
    layers = []
    for i in range(depth):
        j = i // 2
        p = {"norm_mix": norm_mix[i][None], "norm_ffn": norm_ffn[i][None],
             "wg": ffn_w_gate[i].astype(BF16), "wu": ffn_w_up[i].astype(BF16),
             "wd": ffn_w_down[i].astype(BF16)}
        if i % 2 == 0:
            p.update({"w_in": ev_w_in[j].astype(BF16), "conv_w": ev_conv_w[j], "conv_b": ev_conv_b[j][None],
                      "conv_norm": ev_conv_norm[j][None], "q_norm": ev_q_norm[j][None],
                      "k_norm": ev_k_norm[j][None], "lam": ev_lambda[j], "subln": ev_subln[j][None],
                      "w_out": ev_w_out[j].astype(BF16),
                      "lambda_init": 0.8 - 0.6 * math.exp(-0.3 * i)})
        else:
            w_gates = jnp.pad(od_w_in[j][:, ML_MAIN:], ((0, 0), (0, LANES - ML_GATES)))
            nq = ML_HEADS * ML_DK
            w = od_w_in[j]
            p.update({"w_main": jnp.concatenate([w[:, 2 * nq:ML_MAIN], w[:, :nq]], axis=1).astype(BF16),
                      "w_kt": w[:, nq:2 * nq].T.astype(BF16), "w_gates": w_gates.astype(BF16),
                      "bias_col": od_gate_b[j].reshape(ML_GATES, 1),
                      "bias_row": od_gate_b[j].reshape(1, ML_GATES),
                      "out_norm": od_out_norm[j].reshape(1, D_MODEL),
                      "w_out": od_w_out[j].astype(BF16)})
        layers.append(p)

    outs = []
    for x3 in (x_prompt, x_sample):
        b, s, d = x3.shape
        x = x3.reshape(b * s, d)
        for i, p in enumerate(layers):
            x = _even_layer(x, b, s, p) if i % 2 == 0 else _odd_layer(x, b, s, p)
            x = ffn(x, p["norm_ffn"], p["wg"], p["wu"], p["wd"])
        outs.append(x.reshape(b, s, d))
    return tuple(outs)
```

```python
import functools
import math

import jax
import jax.numpy as jnp
from jax import lax
from jax.experimental import pallas as pl
from jax.experimental.pallas import tpu as pltpu

F32 = jnp.float32
BF16 = jnp.bfloat16

D_MODEL = 2048
CONV_CH = D_MODEL // 2
CONV_WIDTH = 31
CONV_PAD = CONV_WIDTH // 2
CONV_HALO = 16
DA_HEAD_DIM = 128
DA_HEADS = D_MODEL // (4 * DA_HEAD_DIM)
DA_QK = DA_HEADS * 2 * DA_HEAD_DIM
ML_HEADS = 4
ML_DV = D_MODEL // ML_HEADS
ML_DK = ML_DV // 2
ML_GATES = 4 * ML_HEADS
ML_MAIN = 2 * ML_HEADS * ML_DK + 2 * ML_HEADS * ML_DV
CHUNK = 128
ROPE_THETA = 10000.0
EPS = 1e-6
LANES = 128
SUBLANES = 8
LOG2E = math.log2(math.e)
SAFE_LOGIT_LOG2 = 64.0
VMEM_LIMIT = 56 * 1024 * 1024


def _params(semantics, vmem_limit=VMEM_LIMIT):
    return pltpu.CompilerParams(dimension_semantics=semantics, vmem_limit_bytes=vmem_limit)


def _rms(x, g):
    ms = jnp.mean(x * x, axis=-1, keepdims=True)
    return x * lax.rsqrt(ms + EPS) * g


def _norm_matmul_kernel(x_ref, g_ref, w_ref, o_ref, xn_ref, w_side_ref=None, w_t_ref=None,
                        o_side_ref=None, o_t_ref=None):
    @pl.when(pl.program_id(1) == 0)
    def _():
        xn = _rms(x_ref[...], g_ref[...]).astype(BF16)
        xn_ref[...] = xn
        if w_side_ref is not None:
            o_side_ref[...] = jnp.dot(xn, w_side_ref[...], preferred_element_type=F32)
            o_t_ref[...] = lax.dot_general(w_t_ref[...], xn, (((1,), (1,)), ((), ())),
                                           preferred_element_type=F32).astype(o_t_ref.dtype)

    o_ref[...] = jnp.dot(xn_ref[...], w_ref[...], preferred_element_type=F32).astype(o_ref.dtype)


def _norm_matmul_side_kernel(x_ref, g_ref, w_ref, w_side_ref, w_t_ref, o_ref, o_side_ref, o_t_ref, xn_ref):
    _norm_matmul_kernel(x_ref, g_ref, w_ref, o_ref, xn_ref, w_side_ref, w_t_ref, o_side_ref, o_t_ref)


def norm_matmul(x, g, w, w_side=None, w_t=None, tm=1024, tn=1024):
    m, d = x.shape
    n = w.shape[1]
    tm, tn = min(tm, m), min(tn, n)
    assert m % tm == 0 and n % tn == 0, (m, n, tm, tn)
    in_specs = [pl.BlockSpec((tm, d), lambda i, j: (i, 0)),
                pl.BlockSpec((1, d), lambda i, j: (0, 0)),
                pl.BlockSpec((d, tn), lambda i, j: (0, j))]
    out_specs = [pl.BlockSpec((tm, tn), lambda i, j: (i, j))]
    out_shape = [jax.ShapeDtypeStruct((m, n), BF16)]
    operands = [x, g, w]
    if w_side is not None:
        ns, nt = w_side.shape[1], w_t.shape[0]
        in_specs += [pl.BlockSpec((d, ns), lambda i, j: (0, 0)), pl.BlockSpec((nt, d), lambda i, j: (0, 0))]
        out_specs += [pl.BlockSpec((tm, ns), lambda i, j: (i, 0)), pl.BlockSpec((nt, tm), lambda i, j: (0, i))]
        out_shape += [jax.ShapeDtypeStruct((m, ns), F32), jax.ShapeDtypeStruct((nt, m), BF16)]
        operands += [w_side, w_t]
    outs = pl.pallas_call(
        _norm_matmul_kernel if w_side is None else _norm_matmul_side_kernel,
        grid=(m // tm, n // tn),
        in_specs=in_specs,
        out_specs=out_specs,
        out_shape=out_shape,
        scratch_shapes=[pltpu.VMEM((tm, d), BF16)],
        compiler_params=_params(("parallel", "arbitrary")),
        name="norm_matmul",
    )(*operands)
    return outs[0] if w_side is None else outs


def _conv_kernel(ac_ref, ap_ref, an_ref, gc_ref, gp_ref, gn_ref, w_ref, b_ref, nrm_ref, o_ref, u_ref,
                 *, ts, tr):
    i = pl.program_id(1)
    n = pl.num_programs(1)

    def glu(a_ref, g_ref):
        return a_ref[0].astype(F32) * jax.nn.sigmoid(g_ref[0].astype(F32))

    u_ref[0, 0:CONV_HALO, :] = jnp.where(i > 0, glu(ap_ref, gp_ref), 0.0)
    u_ref[0, CONV_HALO:CONV_HALO + ts, :] = glu(ac_ref, gc_ref)
    u_ref[0, CONV_HALO + ts:, :] = jnp.where(i < n - 1, glu(an_ref, gn_ref), 0.0)
    rows = ts + 2 * CONV_HALO - SUBLANES
    for sh in range(1, SUBLANES):
        u_ref[sh, 0:rows, :] = u_ref[0, sh:sh + rows, :]

    bias = b_ref[...]
    nrm = nrm_ref[...]
    base = CONV_HALO - CONV_PAD
    for r in range(ts // tr):
        acc = jnp.zeros((tr, CONV_CH), F32)
        for k in range(CONV_WIDTH):
            sh, lo = (base + k) % SUBLANES, r * tr + (base + k) // SUBLANES * SUBLANES
            w_tap = w_ref[k * SUBLANES:(k + 1) * SUBLANES, :]
            acc = acc + u_ref[sh, lo:lo + tr, :] * jnp.concatenate([w_tap] * (tr // SUBLANES), axis=0)
        y = _rms(acc + bias, nrm)
        o_ref[0, r * tr:(r + 1) * tr, :] = (y * jax.nn.sigmoid(y)).astype(o_ref.dtype)


def conv_module(hcat, conv_w, conv_b, conv_norm, ts=256, tr=32):
    b, s, _ = hcat.shape
    assert s % ts == 0 and ts % tr == 0 and ts % CONV_HALO == 0, (s, ts, tr)
    nh = ts // CONV_HALO
    last = s // CONV_HALO - 1
    cur = lambda col: pl.BlockSpec((1, ts, CONV_CH), lambda bi, i: (bi, i, col))
    prev = lambda col: pl.BlockSpec((1, CONV_HALO, CONV_CH),
                                    lambda bi, i: (bi, jnp.maximum(i * nh - 1, 0), col))
    nxt = lambda col: pl.BlockSpec((1, CONV_HALO, CONV_CH),
                                   lambda bi, i: (bi, jnp.minimum((i + 1) * nh, last), col))
    vec = lambda rows: pl.BlockSpec((rows, CONV_CH), lambda bi, i: (0, 0))
    return pl.pallas_call(
        functools.partial(_conv_kernel, ts=ts, tr=tr),
        grid=(b, s // ts),
        in_specs=[cur(0), prev(0), nxt(0), cur(1), prev(1), nxt(1), vec(CONV_WIDTH * SUBLANES), vec(1), vec(1)],
        out_specs=pl.BlockSpec((1, ts, CONV_CH), lambda bi, i: (bi, i, 0)),
        out_shape=jax.ShapeDtypeStruct((b, s, CONV_CH), BF16),
        scratch_shapes=[pltpu.VMEM((SUBLANES, ts + 2 * CONV_HALO, CONV_CH), F32)],
        compiler_params=_params(("parallel", "arbitrary")),
        name="conv_module",
    )(hcat, hcat, hcat, hcat, hcat, hcat, jnp.repeat(conv_w, SUBLANES, axis=0), conv_b, conv_norm)


def _even_in_kernel(x_ref, g_ref, w_ref, cos_ref, sin_ref, qn_ref, kn_ref, o_ref, xn_ref,
                    *, n_steps, q_scale, rh):
    j = pl.program_id(1)

    @pl.when(j == 0)
    def _():
        xn_ref[...] = _rms(x_ref[...], g_ref[...]).astype(BF16)

    d = DA_HEAD_DIM
    tm, tn = o_ref.shape
    q_lo, k_lo, v_lo = 2 * CONV_CH, 2 * CONV_CH + DA_QK, 2 * CONV_CH + 2 * DA_QK

    def norm_rope(y, gain, mul, rows):
        yn = _rms(y, gain)
        rot = pltpu.roll(yn, d // 2, axis=1)
        return (yn * cos_ref[rows, :] + rot * sin_ref[rows, :]) * mul

    def finish(y, col, c0, rows):
        if q_lo <= col < v_lo:
            gain, mul = (qn_ref[...], q_scale) if col < k_lo else (kn_ref[...], 1.0)
            y = jnp.concatenate([norm_rope(y[:, :d], gain, mul, rows), norm_rope(y[:, d:], gain, mul, rows)],
                                axis=1)
        o_ref[rows, c0:c0 + 2 * d] = y.astype(o_ref.dtype)

    for step in range(n_steps):
        @pl.when(j == step)
        def _(step=step):
            pending = None
            for c0 in range(0, tn, 2 * d):
                for r0 in range(0, tm, rh):
                    rows = slice(r0, r0 + rh)
                    y = jnp.dot(xn_ref[rows, :], w_ref[:, c0:c0 + 2 * d], preferred_element_type=F32)
                    if pending is not None:
                        finish(*pending)
                    pending = (y, step * tn + c0, c0, rows)
            finish(*pending)


def even_in_proj(x, g, w, cos, sin, q_norm, k_norm, tm=1024, n_steps=2, rh=512):
    m, dm = x.shape
    n = w.shape[1]
    s = cos.shape[0]
    tm = min(tm, s)
    rh = min(rh, tm)
    tn = n // n_steps
    assert m % tm == 0 and s % tm == 0 and tm % rh == 0, (m, s, tm, rh)
    assert n % n_steps == 0 and tn % (2 * DA_HEAD_DIM) == 0, (n, n_steps)
    tab = pl.BlockSpec((tm, DA_HEAD_DIM), lambda i, j: (i % (s // tm), 0))
    vec = pl.BlockSpec((1, DA_HEAD_DIM), lambda i, j: (0, 0))
    return pl.pallas_call(
        functools.partial(_even_in_kernel, n_steps=n_steps, q_scale=LOG2E / math.sqrt(DA_HEAD_DIM), rh=rh),
        grid=(m // tm, n_steps),
        in_specs=[pl.BlockSpec((tm, dm), lambda i, j: (i, 0)),
                  pl.BlockSpec((1, dm), lambda i, j: (0, 0)),
                  pl.BlockSpec((dm, tn), lambda i, j: (0, j)),
                  tab, tab, vec, vec],
        out_specs=pl.BlockSpec((tm, tn), lambda i, j: (i, j)),
        out_shape=jax.ShapeDtypeStruct((m, n), BF16),
        scratch_shapes=[pltpu.VMEM((tm, dm), BF16)],
        compiler_params=_params(("parallel", "arbitrary")),
        name="even_in_proj",
    )(x, g, w, cos, sin, q_norm, k_norm)


def _attn_kernel(q_ref, k_ref, v_ref, lam_ref, subln_ref, o_ref, m_ref, l_ref, acc_ref,
                 *, lambda_init, stabilize):
    ki = pl.program_id(3)
    d = DA_HEAD_DIM

    @pl.when(ki == 0)
    def _():
        if stabilize:
            m_ref[...] = jnp.full_like(m_ref, -jnp.inf)
        l_ref[...] = jnp.zeros_like(l_ref)
        acc_ref[...] = jnp.zeros_like(acc_ref)

    v = v_ref[0]
    for c in range(2):
        q = q_ref[0, :, c * d:(c + 1) * d]
        k = k_ref[0, :, c * d:(c + 1) * d]
        s = lax.dot_general(q, k, (((1,), (1,)), ((), ())), preferred_element_type=F32)
        if stabilize:
            m_prev = m_ref[c]
            m_new = jnp.maximum(m_prev, jnp.max(s, axis=-1, keepdims=True))
            alpha = jnp.exp2(m_prev - m_new)
            m_ref[c] = m_new
            p = jnp.exp2(s - m_new)
        else:
            p = jnp.exp2(s)
        psum = p[:, 0:LANES]
        for j in range(1, p.shape[1] // LANES):
            psum = psum + p[:, j * LANES:(j + 1) * LANES]
        pv = jnp.dot(p.astype(BF16), v, preferred_element_type=F32)
        if stabilize:
            l_ref[c] = alpha * l_ref[c] + psum
            acc_ref[c] = alpha * acc_ref[c] + pv
        else:
            l_ref[c] += psum
            acc_ref[c] += pv

    @pl.when(ki == pl.num_programs(3) - 1)
    def _():
        lp = lam_ref[...]
        lam = (jnp.exp(jnp.sum(lp[0:1] * lp[1:2], axis=-1, keepdims=True))
               - jnp.exp(jnp.sum(lp[2:3] * lp[3:4], axis=-1, keepdims=True)) + lambda_init)
        l0 = jnp.sum(l_ref[0], axis=-1, keepdims=True)
        l1 = jnp.sum(l_ref[1], axis=-1, keepdims=True)
        o = acc_ref[0] / l0 - lam * (acc_ref[1] / l1)
        o_ref[0] = (_rms(o, subln_ref[...]) * (1.0 - lambda_init)).astype(o_ref.dtype)


def logit_bound_log2(q_norm, k_norm):
    return (LOG2E * math.sqrt(DA_HEAD_DIM)) * jnp.max(jnp.abs(q_norm)) * jnp.max(jnp.abs(k_norm))


def diff_attention(hcat, lam_params, subln, lambda_init, stabilize, tq=1024, tk=4096):
    b, s, n = hcat.shape
    tq, tk = min(tq, s), min(tk, s)
    assert s % tq == 0 and s % tk == 0, (s, tq, tk)
    dv = 2 * DA_HEAD_DIM
    q_col0, k_col0, v_col0 = [(n - r * DA_QK) // dv for r in (3, 2, 1)]
    return pl.pallas_call(
        functools.partial(_attn_kernel, lambda_init=lambda_init, stabilize=stabilize),
        grid=(b, DA_HEADS, s // tq, s // tk),
        in_specs=[pl.BlockSpec((1, tq, dv), lambda bi, h, qi, ki: (bi, qi, q_col0 + h)),
                  pl.BlockSpec((1, tk, dv), lambda bi, h, qi, ki: (bi, ki, k_col0 + h)),
                  pl.BlockSpec((1, tk, dv), lambda bi, h, qi, ki: (bi, ki, v_col0 + h)),
                  pl.BlockSpec((4, DA_HEAD_DIM), lambda bi, h, qi, ki: (0, 0)),
                  pl.BlockSpec((1, dv), lambda bi, h, qi, ki: (0, 0))],
        out_specs=pl.BlockSpec((1, tq, dv), lambda bi, h, qi, ki: (bi, qi, h)),
        out_shape=jax.ShapeDtypeStruct((b, s, DA_QK), BF16),
        scratch_shapes=[pltpu.VMEM((2, tq, 1), F32), pltpu.VMEM((2, tq, LANES), F32),
                        pltpu.VMEM((2, tq, dv), F32)],
        compiler_params=_params(("parallel", "parallel", "parallel", "arbitrary")),
        name="diff_attention",
    )(hcat, hcat, hcat, lam_params, subln)


def _even_out_kernel(x_ref, a_ref, b_ref, w_ref, o_ref):
    ka = a_ref.shape[1]
    y = jnp.dot(a_ref[...], w_ref[0:ka, :], preferred_element_type=F32)
    y = y + jnp.dot(b_ref[...], w_ref[ka:, :], preferred_element_type=F32)
    o_ref[...] = x_ref[...] + y


def even_out(x, y_conv, y_att, w, tm=512):
    m, d = x.shape
    assert m % tm == 0, (m, tm)
    row = lambda width: pl.BlockSpec((tm, width), lambda i: (i, 0))
    return pl.pallas_call(
        _even_out_kernel,
        grid=(m // tm,),
        in_specs=[row(d), row(y_conv.shape[1]), row(y_att.shape[1]),
                  pl.BlockSpec(w.shape, lambda i: (0, 0))],
        out_specs=row(d),
        out_shape=jax.ShapeDtypeStruct((m, d), F32),
        compiler_params=_params(("parallel",)),
        name="even_out",
    )(x, y_conv, y_att, w)


def _odd_out_kernel(x_ref, hf_ref, hb_ref, og_ref, nrm_ref, w_ref, o_ref, z_ref):
    for h in range(ML_HEADS):
        sl = slice(h * ML_DV, (h + 1) * ML_DV)
        hs = hf_ref[:, sl].astype(F32) + hb_ref[:, sl].astype(F32)
        z = _rms(hs, nrm_ref[:, sl]) * jax.nn.sigmoid(og_ref[:, sl].astype(F32))
        z_ref[:, sl] = z.astype(BF16)
    o_ref[...] = x_ref[...] + jnp.dot(z_ref[...], w_ref[...], preferred_element_type=F32)


def odd_out(x, h_fwd, h_bwd, hcat, out_norm, w, tm=512):
    m, d = x.shape
    assert m % tm == 0, (m, tm)
    row = lambda c: pl.BlockSpec((tm, d), lambda i: (i, c))
    return pl.pallas_call(
        _odd_out_kernel,
        grid=(m // tm,),
        in_specs=[row(0), row(0), row(0), row(1),
                  pl.BlockSpec((1, d), lambda i: (0, 0)),
                  pl.BlockSpec(w.shape, lambda i: (0, 0))],
        out_specs=row(0),
        out_shape=jax.ShapeDtypeStruct((m, d), F32),
        scratch_shapes=[pltpu.VMEM((tm, d), BF16)],
        compiler_params=_params(("parallel",)),
        name="odd_out",
    )(x, h_fwd, h_bwd, hcat, out_norm, w)


def _ffn_kernel(x_ref, g_ref, wg_ref, wu_ref, wd_ref, o_ref, xn_ref):
    j = pl.program_id(1)
    half = x_ref.shape[0] // 2

    def hidden_out(xn):
        gate = jnp.dot(xn, wg_ref[...], preferred_element_type=F32)
        up = jnp.dot(xn, wu_ref[...], preferred_element_type=F32)
        hid = (gate * jax.nn.sigmoid(gate) * up).astype(BF16)
        return jnp.dot(hid, wd_ref[...], preferred_element_type=F32)

    @pl.when(j == 0)
    def _():
        x0 = x_ref[0:half, :]
        xn0 = _rms(x0, g_ref[...]).astype(BF16)
        xn_ref[0:half, :] = xn0
        y0 = hidden_out(xn0)
        x1 = x_ref[half:, :]
        xn1 = _rms(x1, g_ref[...]).astype(BF16)
        xn_ref[half:, :] = xn1
        o_ref[0:half, :] = x0 + y0
        o_ref[half:, :] = x1 + hidden_out(xn1)

    @pl.when(j > 0)
    def _():
        o_ref[...] += hidden_out(xn_ref[...])


def ffn(x, g, wg, wu, wd, tm=1024, tf=512):
    m, d = x.shape
    f = wg.shape[1]
    tm = min(tm, m)
    assert m % tm == 0 and f % tf == 0, (m, f, tm, tf)
    vmem = 2 * 2 * tm * d * 4 + tm * d * 2 + 2 * 3 * d * tf * 2 + 3 * tm * tf * 4
    return pl.pallas_call(
        _ffn_kernel,
        grid=(m // tm, f // tf),
        in_specs=[pl.BlockSpec((tm, d), lambda i, j: (i, 0)),
                  pl.BlockSpec((1, d), lambda i, j: (0, 0)),
                  pl.BlockSpec((d, tf), lambda i, j: (0, j)),
                  pl.BlockSpec((d, tf), lambda i, j: (0, j)),
                  pl.BlockSpec((tf, d), lambda i, j: (j, 0))],
        out_specs=pl.BlockSpec((tm, d), lambda i, j: (i, 0)),
        out_shape=jax.ShapeDtypeStruct((m, d), F32),
        scratch_shapes=[pltpu.VMEM((tm, d), BF16)],
        compiler_params=_params(("parallel", "arbitrary"), vmem_limit=max(VMEM_LIMIT, vmem)),
        name="ffn",
    )(x, g, wg, wu, wd)


def _split3(x):
    x1 = x.astype(BF16)
    r = x - x1.astype(F32)
    x2 = r.astype(BF16)
    x3 = (r - x2.astype(F32)).astype(BF16)
    return x1, x2, x3


def _mlstm_kernel(qf_ref, ktf_ref, vf_ref, gf_ref, gtf_ref, qb_ref, ktb_ref, vb_ref, gb_ref, gtb_ref,
                  bc_ref, br_ref, hf_ref, hb_ref, c_ref, m_ref):
    @pl.when(pl.program_id(1) == 0)
    def _():
        c_ref[...] = jnp.zeros_like(c_ref)
        m_ref[...] = jnp.zeros_like(m_ref)

    _mlstm_direction(qf_ref, ktf_ref, vf_ref, gf_ref, gtf_ref, bc_ref, br_ref, hf_ref,
                     c_ref.at[0], m_ref.at[0], reverse=False)
    _mlstm_direction(qb_ref, ktb_ref, vb_ref, gb_ref, gtb_ref, bc_ref, br_ref, hb_ref,
                     c_ref.at[1], m_ref.at[1], reverse=True)


def _mlstm_direction(q_ref, kt_ref, v_ref, g_ref, gt_ref, bc_ref, br_ref, h_ref, c_ref, m_ref,
                     *, reverse):
    L = CHUNK
    t_idx = lax.broadcasted_iota(jnp.int32, (L, L), 0)
    s_idx = lax.broadcasted_iota(jnp.int32, (L, L), 1)
    mask = (s_idx >= t_idx) if reverse else (s_idx <= t_idx)
    mask_t = (s_idx <= t_idx) if reverse else (s_idx >= t_idx)
    tri_col = jnp.where(mask, 1.0, 0.0).astype(BF16)
    tri_row = jnp.where(mask_t, 1.0, 0.0).astype(BF16)
    last = 0 if reverse else L - 1

    g_col = g_ref[0][:, 0:ML_GATES] + br_ref[...]
    g_row = gt_ref[0] + bc_ref[...]
    lf_col = jax.nn.log_sigmoid(g_col)
    lf_row = jax.nn.log_sigmoid(g_row)
    bcum_col = sum(jnp.dot(tri_col, p, preferred_element_type=F32) for p in _split3(lf_col))
    bcum_row = sum(jnp.dot(p, tri_row, preferred_element_type=F32) for p in _split3(lf_row))

    it = 2 if reverse else 0
    for h in range(ML_HEADS):
        ci = it * ML_HEADS + h
        cf = (it + 1) * ML_HEADS + h
        i_row = g_row[ci:ci + 1, :]
        b_col = bcum_col[:, cf:cf + 1]
        b_row = bcum_row[cf:cf + 1, :]
        total = b_col[last:last + 1, :]
        m_prev = m_ref[h]

        q = q_ref[0, :, h * ML_DK:(h + 1) * ML_DK]
        kt = kt_ref[h * ML_DK:(h + 1) * ML_DK, :]
        v_aug = jnp.concatenate([v_ref[0, :, h * ML_DV:(h + 1) * ML_DV], jnp.ones((L, LANES), BF16)], axis=1)

        dlog = jnp.where(mask, b_col - b_row + i_row, -jnp.inf)
        inter = b_col + m_prev
        mt = jnp.maximum(inter, jnp.max(dlog, axis=-1, keepdims=True))
        dmat = jnp.exp(dlog - mt)
        qk = jnp.dot(q, kt, preferred_element_type=F32)
        sc = qk * dmat * (1.0 / math.sqrt(ML_DK))
        w_int = jnp.exp(inter - mt)
        qc = jnp.dot(q, c_ref[h].astype(BF16), preferred_element_type=F32)
        full = jnp.dot(sc.astype(BF16), v_aug, preferred_element_type=F32) + w_int * qc
        den = full[:, ML_DV:]
        inv = 1.0 / jnp.maximum(jnp.abs(den), jnp.exp(-mt))
        hout = full[:, :ML_DV] * jnp.concatenate([inv] * (ML_DV // LANES), axis=1)
        h_ref[0, :, h * ML_DV:(h + 1) * ML_DV] = hout.astype(h_ref.dtype)

        m_new = mt[last:last + 1, :]
        w_s = jnp.exp(total - b_row + i_row - m_new) * (1.0 / math.sqrt(ML_DK))
        decay = jnp.exp(total + m_prev - m_new)
        kwt = (kt.astype(F32) * w_s).astype(BF16)
        c_ref[h] = decay * c_ref[h] + jnp.dot(kwt, v_aug, preferred_element_type=F32)
        m_ref[h] = m_new


def mlstm_scan(hcat, k_t, gates, gates_t, bias_col, bias_row):
    b, s, _ = hcat.shape
    assert s % CHUNK == 0, (s, CHUNK)
    nc = s // CHUNK
    nq = ML_HEADS * ML_DK
    nv = ML_HEADS * ML_DV
    q_col = 2 * nv // nq
    fwd = lambda c: c
    bwd = lambda c: nc - 1 - c

    def direction_specs(pos):
        return [pl.BlockSpec((1, CHUNK, nq), lambda bi, c: (bi, pos(c), q_col)),
                pl.BlockSpec((nq, CHUNK), lambda bi, c: (0, bi * nc + pos(c))),
                pl.BlockSpec((1, CHUNK, nv), lambda bi, c: (bi, pos(c), 0)),
                pl.BlockSpec((1, CHUNK, LANES), lambda bi, c: (bi, pos(c), 0)),
                pl.BlockSpec((1, ML_GATES, CHUNK), lambda bi, c: (bi, 0, pos(c)))]

    out_spec = lambda pos: pl.BlockSpec((1, CHUNK, nv), lambda bi, c: (bi, pos(c), 0))
    operands = (hcat, k_t, hcat, gates, gates_t)
    return pl.pallas_call(
        _mlstm_kernel,
        grid=(b, nc),
        in_specs=direction_specs(fwd) + direction_specs(bwd)
        + [pl.BlockSpec((ML_GATES, 1), lambda bi, c: (0, 0)),
           pl.BlockSpec((1, ML_GATES), lambda bi, c: (0, 0))],
        out_specs=[out_spec(fwd), out_spec(bwd)],
        out_shape=[jax.ShapeDtypeStruct((b, s, nv), BF16)] * 2,
        scratch_shapes=[pltpu.VMEM((2, ML_HEADS, ML_DK, ML_DV + LANES), F32),
                        pltpu.VMEM((2, ML_HEADS, 1, 1), F32)],
        compiler_params=_params(("parallel", "arbitrary")),
        name="mlstm",
    )(*operands, *operands, bias_col, bias_row)


def _rope_tables(s):
    d = DA_HEAD_DIM
    inv = 1.0 / (ROPE_THETA ** (jnp.arange(0, d, 2, dtype=F32) / d))
    ang = jnp.arange(s, dtype=F32)[:, None] * inv[None, :]
    cos, sin = jnp.cos(ang), jnp.sin(ang)
    return jnp.concatenate([cos, cos], axis=-1), jnp.concatenate([-sin, sin], axis=-1)


def _even_layer(x, b, s, p):
    m = b * s
    cos, sin = _rope_tables(s)
    hcat = even_in_proj(x, p["norm_mix"], p["w_in"], cos, sin, p["q_norm"], p["k_norm"]).reshape(b, s, -1)
    y_conv = conv_module(hcat, p["conv_w"], p["conv_b"], p["conv_norm"])
    attn = lambda stabilize: functools.partial(diff_attention, lam_params=p["lam"], subln=p["subln"],
                                               lambda_init=p["lambda_init"], stabilize=stabilize)
    y_att = lax.cond(logit_bound_log2(p["q_norm"], p["k_norm"]) < SAFE_LOGIT_LOG2,
                     attn(False), attn(True), hcat)
    return even_out(x, y_conv.reshape(m, -1), y_att.reshape(m, -1), p["w_out"])


def _odd_layer(x, b, s, p):
    m = b * s
    hcat, gates, k_t = norm_matmul(x, p["norm_mix"], p["w_main"], w_side=p["w_gates"], w_t=p["w_kt"],
                                   tn=p["w_main"].shape[1] // 4)
    gates = gates.reshape(b, s, LANES)
    gates_t = jnp.swapaxes(gates[:, :, :ML_GATES], 1, 2)
    h_fwd, h_bwd = mlstm_scan(hcat.reshape(b, s, -1), k_t, gates, gates_t, p["bias_col"], p["bias_row"])
    return odd_out(x, h_fwd.reshape(m, -1), h_bwd.reshape(m, -1), hcat, p["out_norm"], p["w_out"])


def kernel(x_prompt, x_sample, norm_mix, norm_ffn, ev_w_in, ev_conv_w, ev_conv_b, ev_conv_norm, ev_q_norm, ev_k_norm, ev_lambda, ev_subln, ev_w_out, od_w_in, od_gate_b, od_out_norm, od_w_out, ffn_w_gate, ffn_w_up, ffn_w_down):
    depth = norm_mix.shape[0]
    layers = []
    for i in range(depth):
        j = i // 2
        p = {"norm_mix": norm_mix[i][None], "norm_ffn": norm_ffn[i][None],
             "wg": ffn_w_gate[i].astype(BF16), "wu": ffn_w_up[i].astype(BF16),
             "wd": ffn_w_down[i].astype(BF16)}
        if i % 2 == 0:
            p.update({"w_in": ev_w_in[j].astype(BF16), "conv_w": ev_conv_w[j], "conv_b": ev_conv_b[j][None],
                      "conv_norm": ev_conv_norm[j][None], "q_norm": ev_q_norm[j][None],
                      "k_norm": ev_k_norm[j][None], "lam": ev_lambda[j], "subln": ev_subln[j][None],
                      "w_out": ev_w_out[j].astype(BF16),
                      "lambda_init": 0.8 - 0.6 * math.exp(-0.3 * i)})
        else:
            w_gates = jnp.pad(od_w_in[j][:, ML_MAIN:], ((0, 0), (0, LANES - ML_GATES)))
            nq = ML_HEADS * ML_DK
            w = od_w_in[j]
            p.update({"w_main": jnp.concatenate([w[:, 2 * nq:ML_MAIN], w[:, :nq]], axis=1).astype(BF16),
                      "w_kt": w[:, nq:2 * nq].T.astype(BF16), "w_gates": w_gates.astype(BF16),
                      "bias_col": od_gate_b[j].reshape(ML_GATES, 1),
                      "bias_row": od_gate_b[j].reshape(1, ML_GATES),
                      "out_norm": od_out_norm[j].reshape(1, D_MODEL),
                      "w_out": od_w_out[j].astype(BF16)})
        layers.append(p)

    outs = []
    for x3 in (x_prompt, x_sample):
        b, s, d = x3.shape
        x = x3.reshape(b * s, d)
        for i, p in enumerate(layers):
            x = _even_layer(x, b, s, p) if i % 2 == 0 else _odd_layer(x, b, s, p)
            x = ffn(x, p["norm_ffn"], p["wg"], p["wu"], p["wd"])
        outs.append(x.reshape(b, s, d))
    return tuple(outs)
```
